```python
import math
import jax, jax.numpy as jnp
from jax import lax
import numpy as np

D_MODEL = 1024
BATCH = 16
SEQ = 4096
DEPTH = 2

N_MIXERS = 2
N_CONV_LAYERS = (DEPTH + 1) // 2
N_ATTN_LAYERS = DEPTH // 2
CONV_WIDTH = 3
DILATED_GROUPS = ((128, 1), (512, 4), (2048, 16))
N_DIL_GROUPS = len(DILATED_GROUPS)
ATTN_HEADS = 8
HEAD_DIM = 64
ATTN_OUT_DIM = ATTN_HEADS * HEAD_DIM
ATTN_IN_DIM = N_DIL_GROUPS * 3 * ATTN_OUT_DIM
ROT_DIM = HEAD_DIM // 4
ROPE_THETA = 500000.0
N_GROUPS = 4
EXPERTS_PER_GROUP = 8
N_EXPERTS = N_GROUPS * EXPERTS_PER_GROUP
TOP_K_EXPERTS = 2
D_EXPERT = 512
MOE_BLOCK = 512
NORM_EPS = 1e-6

kernel_name = "hybrid_shortconv_dilatedattn_hmoe_adaln"


def rms_norm(x, g):
    xf = x.astype(jnp.float32)
    y = xf * lax.rsqrt(jnp.mean(xf * xf, axis=-1, keepdims=True) + NORM_EPS)
    return (y * g.astype(jnp.float32)).astype(x.dtype)


def modulate(h, shift, scale):
    return h * (1.0 + scale[:, None, :]) + shift[:, None, :]


def short_conv_mixer(h, w_in, w_conv, w_out):
    d = h.shape[-1]
    b_gate, c_gate, u = jnp.split(h @ w_in, 3, axis=-1)
    z = c_gate * u
    zc = lax.conv_general_dilated(
        z, w_conv[:, None, :], window_strides=(1,), padding=[(CONV_WIDTH - 1, 0)],
        dimension_numbers=("NWC", "WIO", "NWC"), feature_group_count=d)
    return (b_gate * zc) @ w_out


def rotary_tables(seq):
    pos = jnp.arange(seq, dtype=jnp.float32)
    inv_freq = jnp.power(jnp.float32(ROPE_THETA),
                         -jnp.arange(0, ROT_DIM, 2, dtype=jnp.float32) / ROT_DIM)
    ang = pos[:, None] * inv_freq[None, :]
    return jnp.cos(ang)[None, :, None, :], jnp.sin(ang)[None, :, None, :]


def apply_partial_rope(t, cos, sin):
    half = ROT_DIM // 2
    rot = t[..., :ROT_DIM].astype(jnp.float32)
    x1, x2 = rot[..., :half], rot[..., half:]
    r = jnp.concatenate([x1 * cos - x2 * sin, x2 * cos + x1 * sin], axis=-1)
    return jnp.concatenate([r.astype(t.dtype), t[..., ROT_DIM:]], axis=-1)


def dilated_window_attention(q, k, v, dil, span):
    b, s, h, e = q.shape
    L = s // dil
    blk = span
    nb = -(-L // blk)
    pad = nb * blk - L

    def split(t):
        t = t.reshape(b, L, dil, h, e)
        t = jnp.pad(t, ((0, 0), (0, pad), (0, 0), (0, 0), (0, 0)))
        return t.reshape(b, nb, blk, dil, h, e)

    def banded(t):
        t_ext = jnp.pad(t, ((0, 0), (1, 0), (0, 0), (0, 0), (0, 0), (0, 0)))
        return jnp.concatenate([t_ext[:, :-1], t_ext[:, 1:]], axis=2)

    qb = split(q)
    kw = banded(split(k))
    vw = banded(split(v))
    scores = jnp.einsum("bnqrhe,bnkrhe->bnrhqk", qb, kw).astype(jnp.float32) * (HEAD_DIM ** -0.5)
    qi = jnp.arange(blk)[:, None]
    kj = jnp.arange(2 * blk)[None, :]
    dist = qi + blk - kj
    band = (dist >= 0) & (dist <= span)
    valid_start = (jnp.arange(nb)[:, None, None] > 0) | (kj[None] >= blk)
    mask = (band[None] & valid_start)[None, :, None, None]
    scores = jnp.where(mask, scores, -jnp.inf)
    m = jnp.max(scores, axis=-1, keepdims=True)
    p = jnp.exp(scores - m)
    den = jnp.sum(p, axis=-1, keepdims=True)
    lse = (m + jnp.log(den))[..., 0]
    out = jnp.einsum("bnrhqk,bnkrhe->bnqrhe", p / den, vw.astype(jnp.float32))
    out = out.reshape(b, nb * blk, dil, h, e)[:, :L].reshape(b, s, h, e)
    lse = jnp.transpose(lse, (0, 1, 4, 2, 3)).reshape(b, nb * blk, dil, h)[:, :L].reshape(b, s, h)
    return out, lse


def dilated_attention_mixer(h, w_in, w_out):
    b, s, _ = h.shape
    qkv = (h @ w_in).reshape(b, s, N_DIL_GROUPS, 3, ATTN_HEADS, HEAD_DIM)
    cos, sin = rotary_tables(s)
    outs, lses = [], []
    for g, (window, dil) in enumerate(DILATED_GROUPS):
        q = apply_partial_rope(qkv[:, :, g, 0], cos, sin)
        k = apply_partial_rope(qkv[:, :, g, 1], cos, sin)
        o, lse = dilated_window_attention(q, k, qkv[:, :, g, 2], dil, window // dil)
        outs.append(o)
        lses.append(lse)
    wts = jax.nn.softmax(jnp.stack(lses, axis=0), axis=0)
    o = jnp.einsum("gbshe,gbsh->bshe", jnp.stack(outs, axis=0), wts)
    return o.astype(h.dtype).reshape(b, s, ATTN_OUT_DIM) @ w_out


def expert_dispatch(hf, eid, wts, w_gate, w_up, w_down):
    t, d = hf.shape
    k = eid.shape[1]
    tk = t * k
    n_blocks = -(-tk // MOE_BLOCK) + N_EXPERTS
    rows = n_blocks * MOE_BLOCK
    flat_e = eid.reshape(-1)
    flat_tok = jnp.repeat(jnp.arange(t, dtype=jnp.int32), k)
    flat_w = wts.reshape(-1)
    order = jnp.argsort(flat_e)
    se, stok, sw = flat_e[order], flat_tok[order], flat_w[order]
    counts = jnp.bincount(flat_e, length=N_EXPERTS)
    starts = jnp.cumsum(counts) - counts
    pcounts = (counts + MOE_BLOCK - 1) // MOE_BLOCK * MOE_BLOCK
    pends = jnp.cumsum(pcounts)
    pstarts = pends - pcounts
    dest = pstarts[se] + (jnp.arange(tk) - starts[se])
    row_tok = jnp.full((rows,), t, dtype=jnp.int32).at[dest].set(stok)
    row_w = jnp.zeros((rows,), jnp.float32).at[dest].set(sw)
    block_e = jnp.clip(jnp.searchsorted(pends, jnp.arange(n_blocks) * MOE_BLOCK, side="right"),
                       0, N_EXPERTS - 1)
    hf_pad = jnp.concatenate([hf, jnp.zeros((1, d), hf.dtype)], axis=0)
    xs = hf_pad[row_tok].reshape(n_blocks, MOE_BLOCK, d)

    def expert_block(args):
        xb, e = args
        return (jax.nn.silu(xb @ w_gate[e]) * (xb @ w_up[e])) @ w_down[e]

    ys = lax.map(expert_block, (xs, block_e)).reshape(rows, d)
    out = jnp.zeros((t + 1, d), ys.dtype).at[row_tok].add(row_w[:, None].astype(ys.dtype) * ys)
    return out[:t]


def hierarchical_moe(h, w_grp, b_grp, w_exp_r, b_exp_r, w_gate, w_up, w_down):
    b, s, d = h.shape
    hf = h.reshape(b * s, d)
    hf32 = hf.astype(jnp.float32)
    grp_prob = jax.nn.softmax(hf32 @ w_grp.astype(jnp.float32) + b_grp.astype(jnp.float32), axis=-1)
    grp_p, grp_idx = lax.top_k(grp_prob, 1)
    exp_logits = jnp.einsum("td,gde->tge", hf32, w_exp_r.astype(jnp.float32)) \
        + b_exp_r.astype(jnp.float32)[None]
    sel = jnp.take_along_axis(exp_logits, grp_idx[:, :, None], axis=1)[:, 0]
    top_p, top_i = lax.top_k(jax.nn.softmax(sel, axis=-1), TOP_K_EXPERTS)
    wts = grp_p * top_p / jnp.sum(top_p, axis=-1, keepdims=True)
    eid = grp_idx * EXPERTS_PER_GROUP + top_i
    return expert_dispatch(hf, eid, wts, w_gate, w_up, w_down).reshape(b, s, d)


def setup_inputs(seed: int = 0) -> dict:
    key = jax.random.key(seed)
    ks = jax.random.split(key, 20)
    d = D_MODEL
    nrm = lambda k, shape, scale: jax.random.normal(k, shape, jnp.float32) * scale
    return {
        "x": nrm(ks[0], (BATCH, SEQ, d), 1.0),
        "c": nrm(ks[1], (BATCH, d), 1.0),
        "norm_mix_g": 1.0 + nrm(ks[2], (DEPTH, d), 0.02),
        "norm_ffn_g": 1.0 + nrm(ks[3], (DEPTH, d), 0.02),
        "ada_w": nrm(ks[4], (DEPTH, d, 6 * d), 0.5 * d ** -0.5),
        "ada_b": nrm(ks[5], (DEPTH, 6 * d), 0.02),
        "conv_in_w": nrm(ks[6], (N_CONV_LAYERS, d, 3 * d), d ** -0.5),
        "conv_w": nrm(ks[7], (N_CONV_LAYERS, CONV_WIDTH, d), CONV_WIDTH ** -0.5),
        "conv_out_w": nrm(ks[8], (N_CONV_LAYERS, d, d), d ** -0.5),
        "attn_in_w": nrm(ks[9], (N_ATTN_LAYERS, d, ATTN_IN_DIM), d ** -0.5),
        "attn_out_w": nrm(ks[10], (N_ATTN_LAYERS, ATTN_OUT_DIM, d), ATTN_OUT_DIM ** -0.5),
        "router_grp_w": nrm(ks[11], (DEPTH, d, N_GROUPS), d ** -0.5),
        "router_grp_b": nrm(ks[12], (DEPTH, N_GROUPS), 0.01),
        "router_exp_w": nrm(ks[13], (DEPTH, N_GROUPS, d, EXPERTS_PER_GROUP), d ** -0.5),
        "router_exp_b": nrm(ks[14], (DEPTH, N_GROUPS, EXPERTS_PER_GROUP), 0.01),
        "exp_gate_w": nrm(ks[15], (DEPTH, N_EXPERTS, d, D_EXPERT), d ** -0.5),
        "exp_up_w": nrm(ks[16], (DEPTH, N_EXPERTS, d, D_EXPERT), d ** -0.5),
        "exp_down_w": nrm(ks[17], (DEPTH, N_EXPERTS, D_EXPERT, d), D_EXPERT ** -0.5),
        "final_norm_g": 1.0 + nrm(ks[18], (d,), 0.02),
    }


def reference(x, c, norm_mix_g, norm_ffn_g, ada_w, ada_b, conv_in_w, conv_w, conv_out_w,
              attn_in_w, attn_out_w, router_grp_w, router_grp_b, router_exp_w, router_exp_b,
              exp_gate_w, exp_up_w, exp_down_w, final_norm_g):
    c_act = jax.nn.silu(c)
    for i in range(DEPTH):
        mod = c_act @ ada_w[i] + ada_b[i]
        sh1, sc1, g1, sh2, sc2, g2 = jnp.split(mod, 6, axis=-1)
        h = modulate(rms_norm(x, norm_mix_g[i]), sh1, sc1)
        j = i // N_MIXERS
        if i % N_MIXERS == 0:
            y = short_conv_mixer(h, conv_in_w[j], conv_w[j], conv_out_w[j])
        else:
            y = dilated_attention_mixer(h, attn_in_w[j], attn_out_w[j])
        x = x + g1[:, None, :] * y
        h = modulate(rms_norm(x, norm_ffn_g[i]), sh2, sc2)
        x = x + g2[:, None, :] * hierarchical_moe(
            h, router_grp_w[i], router_grp_b[i], router_exp_w[i], router_exp_b[i],
            exp_gate_w[i], exp_up_w[i], exp_down_w[i])
    return rms_norm(x, final_norm_g)
```

```python
import functools

import jax
import jax.numpy as jnp
from jax import lax
from jax.experimental import pallas as pl
from jax.experimental.pallas import tpu as pltpu

CONV_WIDTH = 3
DILATED_GROUPS = ((128, 1), (512, 4), (2048, 16))
N_DIL_GROUPS = len(DILATED_GROUPS)
ATTN_HEADS = 8
HEAD_DIM = 64
ATTN_OUT_DIM = ATTN_HEADS * HEAD_DIM
ATTN_IN_DIM = N_DIL_GROUPS * 3 * ATTN_OUT_DIM
ROT_DIM = HEAD_DIM // 4
ROPE_THETA = 500000.0
N_GROUPS = 4
EXPERTS_PER_GROUP = 8
N_EXPERTS = N_GROUPS * EXPERTS_PER_GROUP
NORM_EPS = 1e-6
ATTN_SPAN = 128

V7X_LANES = 128
V7X_VMEM_LIMIT_BYTES = 56 * 1024 * 1024

ROW_TILE = 512
CONV_COL_BLOCK = 256
ROUTE_TILE = 1024
MOE_BLOCK = 512
ATTN_Q_TILE = 256
ROUTER_LANE_OFFSET = 8

F32 = jnp.float32
BF16 = jnp.bfloat16


def _params(n_axes):
    return pltpu.CompilerParams(dimension_semantics=("arbitrary",) * n_axes,
                                vmem_limit_bytes=V7X_VMEM_LIMIT_BYTES)


def _norm_mod(x, g, shift, scale):
    ms = jnp.mean(x * x, axis=-1, keepdims=True)
    y = x * lax.rsqrt(ms + NORM_EPS) * g
    return y * (1.0 + scale) + shift


def _pack_bf16_pair(a, b):
    return pltpu.pack_elementwise([a, b], packed_dtype=BF16)


def _unpack_bf16_pair(w):
    a = pltpu.unpack_elementwise(w, index=0, packed_dtype=BF16, unpacked_dtype=F32)
    b = pltpu.unpack_elementwise(w, index=1, packed_dtype=BF16, unpacked_dtype=F32)
    return a, b


def _ffn_input_and_logits(x1, mod, g2_ref, wr_ref, br_ref, h2_ref, lg_ref):
    d = x1.shape[-1]
    h2 = _norm_mod(x1, g2_ref[...], mod[3:4], mod[4:5])
    h2_ref[...] = _pack_bf16_pair(h2[:, : d // 2], h2[:, d // 2:])
    lg_ref[...] = jnp.dot(h2.astype(BF16), wr_ref[...], preferred_element_type=F32) + br_ref[...]


def _ada_kernel(c_ref, w_ref, b_ref, o_ref):
    c = c_ref[...]
    ca = c * jax.nn.sigmoid(c)
    o_ref[0] = jnp.dot(ca, w_ref[0], preferred_element_type=F32,
                       precision=lax.Precision.HIGHEST) + b_ref[0]


def _ada_modulation(c, ada_w, ada_b):
    depth, d, n = ada_w.shape
    b = c.shape[0]
    tn = d
    return pl.pallas_call(
        _ada_kernel,
        out_shape=jax.ShapeDtypeStruct((depth, b, n), F32),
        grid=(depth, n // tn),
        in_specs=[pl.BlockSpec((b, d), lambda i, j: (0, 0)),
                  pl.BlockSpec((1, d, tn), lambda i, j: (i, 0, j)),
                  pl.BlockSpec((1, 1, tn), lambda i, j: (i, 0, j))],
        out_specs=pl.BlockSpec((1, b, tn), lambda i, j: (i, 0, j)),
        compiler_params=_params(2),
        name="ada_modulation",
    )(c, ada_w, ada_b.reshape(depth, 1, n))


def _conv_mixer_kernel(x_ref, mod_ref, g_ref, win_ref, cw_ref, wout_ref, g2_ref, wr_ref, br_ref,
                       x1_ref, h2_ref, lg_ref, carry_ref, v_ref, *, cb):
    ts, d = x_ref.shape[1], x_ref.shape[2]

    @pl.when(pl.program_id(1) == 0)
    def _():
        carry_ref[...] = jnp.zeros_like(carry_ref)

    x = x_ref[0]
    mod = mod_ref[0]
    h = _norm_mod(x, g_ref[...], mod[0:1], mod[1:2]).astype(BF16)
    row = lax.broadcasted_iota(jnp.int32, (ts, cb), 0)
    for j in range(d // cb):
        lo = j * cb
        bj = jnp.dot(h, win_ref[:, lo:lo + cb], preferred_element_type=F32)
        cj = jnp.dot(h, win_ref[:, d + lo:d + lo + cb], preferred_element_type=F32)
        uj = jnp.dot(h, win_ref[:, 2 * d + lo:2 * d + lo + cb], preferred_element_type=F32)
        z = cj * uj
        prev = carry_ref[:, lo:lo + cb]
        z1 = jnp.where(row == 0, prev[7:8], pltpu.roll(z, 1, 0))
        z2 = jnp.where(row == 0, prev[6:7], jnp.where(row == 1, prev[7:8], pltpu.roll(z, 2, 0)))
        zc = cw_ref[0:1, lo:lo + cb] * z2 + cw_ref[1:2, lo:lo + cb] * z1 + cw_ref[2:3, lo:lo + cb] * z
        carry_ref[:, lo:lo + cb] = z[ts - 8:ts]
        v_ref[:, lo:lo + cb] = (bj * zc).astype(BF16)
    y = jnp.dot(v_ref[...], wout_ref[...], preferred_element_type=F32)
    x1 = x + mod[2:3] * y
    x1_ref[0] = x1
    _ffn_input_and_logits(x1, mod, g2_ref, wr_ref, br_ref, h2_ref, lg_ref)


def _conv_mixer(x, mod, g_mix, w_in, conv_w, w_out, g_ffn, wr, br):
    b, s, d = x.shape
    ts = min(ROW_TILE, s)
    cb = min(CONV_COL_BLOCK, d)
    nst = s // ts
    const = lambda bi, si: (0, 0)
    return pl.pallas_call(
        functools.partial(_conv_mixer_kernel, cb=cb),
        out_shape=(jax.ShapeDtypeStruct((b, s, d), F32),
                   jax.ShapeDtypeStruct((b * s, d // 2), jnp.int32),
                   jax.ShapeDtypeStruct((b * s, V7X_LANES), F32)),
        grid=(b, nst),
        in_specs=[pl.BlockSpec((1, ts, d), lambda bi, si: (bi, si, 0)),
                  pl.BlockSpec((1, 6, d), lambda bi, si: (bi, 0, 0)),
                  pl.BlockSpec((1, d), const),
                  pl.BlockSpec((d, 3 * d), const),
                  pl.BlockSpec((CONV_WIDTH, d), const),
                  pl.BlockSpec((d, d), const),
                  pl.BlockSpec((1, d), const),
                  pl.BlockSpec((d, V7X_LANES), const),
                  pl.BlockSpec((1, V7X_LANES), const)],
        out_specs=(pl.BlockSpec((1, ts, d), lambda bi, si: (bi, si, 0)),
                   pl.BlockSpec((ts, d // 2), lambda bi, si: (bi * nst + si, 0)),
                   pl.BlockSpec((ts, V7X_LANES), lambda bi, si: (bi * nst + si, 0))),
        scratch_shapes=[pltpu.VMEM((8, d), F32), pltpu.VMEM((ts, d), BF16)],
        compiler_params=_params(2),
        name="conv_mixer",
    )(x, mod, g_mix, w_in, conv_w, w_out, g_ffn, wr, br)


def _route_kernel(lg_ref, tri_ref, plan_ref, cnt_ref, run_ref, pst_ref, *, blk):
    ph, i = pl.program_id(0), pl.program_id(1)
    tr = lg_ref.shape[0]
    lanes = lg_ref.shape[1]

    @pl.when((ph == 0) & (i == 0))
    def _():
        run_ref[...] = jnp.zeros_like(run_ref)

    lg = lg_ref[...]
    lane = lax.broadcasted_iota(jnp.int32, (tr, lanes), 1)
    big = jnp.int32(lanes)
    neg = jnp.float32(-jnp.inf)
    gmask = lane < N_GROUPS
    gl = jnp.where(gmask, lg, neg)
    mg = jnp.max(gl, axis=1, keepdims=True)
    gidx = jnp.min(jnp.where(gl == mg, lane, big), axis=1, keepdims=True)
    den_g = jnp.sum(jnp.where(gmask, jnp.exp(lg - mg), 0.0), axis=1, keepdims=True)
    grp_p = 1.0 / den_g
    lo = ROUTER_LANE_OFFSET + EXPERTS_PER_GROUP * gidx
    el = jnp.where((lane >= lo) & (lane < lo + EXPERTS_PER_GROUP), lg, neg)
    l1 = jnp.max(el, axis=1, keepdims=True)
    i1 = jnp.min(jnp.where(el == l1, lane, big), axis=1, keepdims=True)
    el2 = jnp.where(lane == i1, neg, el)
    l2 = jnp.max(el2, axis=1, keepdims=True)
    i2 = jnp.min(jnp.where(el2 == l2, lane, big), axis=1, keepdims=True)
    r = jnp.exp(l2 - l1)
    w1 = grp_p / (1.0 + r)
    w2 = grp_p * r / (1.0 + r)
    oh1 = (lane == i1 - ROUTER_LANE_OFFSET).astype(F32)
    oh2 = (lane == i2 - ROUTER_LANE_OFFSET).astype(F32)
    tot1 = jnp.sum(oh1, axis=0, keepdims=True)
    tot2 = jnp.sum(oh2, axis=0, keepdims=True)
    run = run_ref[0:1, :]

    @pl.when(ph == 0)
    def _():
        new_run = run + tot1 + tot2
        run_ref[...] = jnp.broadcast_to(new_run, run_ref.shape)

        @pl.when(i == pl.num_programs(1) - 1)
        def _():
            cnt = jnp.broadcast_to(new_run, run_ref.shape)
            cnt_ref[...] = cnt
            padded = jnp.floor((cnt + (blk - 1)) / blk) * blk
            l8 = lax.broadcasted_iota(jnp.int32, cnt.shape, 1)
            csum = padded
            k = 1
            while k < N_EXPERTS:
                csum = csum + jnp.where(l8 >= k, pltpu.roll(csum, k, 1), 0.0)
                k *= 2
            pst_ref[...] = csum - padded
            run_ref[...] = jnp.zeros_like(run_ref)

    @pl.when(ph == 1)
    def _():
        oh = jnp.concatenate([oh1, oh2], axis=1).astype(BF16)
        before = jnp.dot(tri_ref[...], oh, preferred_element_type=F32)
        base = pst_ref[0:1, :] + run
        pos1 = jnp.sum(oh1 * (base + before[:, :lanes]), axis=1, keepdims=True)
        pos2 = jnp.sum(oh2 * (base + tot1 + before[:, lanes:]), axis=1, keepdims=True)
        run_ref[...] = jnp.broadcast_to(run + tot1 + tot2, run_ref.shape)
        plan_ref[...] = jnp.where(lane == 0, pos1, jnp.where(lane == 1, pos2,
                                  jnp.where(lane == 2, w1, jnp.where(lane == 3, w2, 0.0))))


def _route(logits):
    t, lanes = logits.shape
    tr = min(ROUTE_TILE, t)
    nt = t // tr
    tri = (jnp.arange(tr)[None, :] < jnp.arange(tr)[:, None]).astype(BF16)
    return pl.pallas_call(
        functools.partial(_route_kernel, blk=MOE_BLOCK),
        out_shape=(jax.ShapeDtypeStruct((t, lanes), F32),
                   jax.ShapeDtypeStruct((8, lanes), F32)),
        grid=(2, nt),
        in_specs=[pl.BlockSpec((tr, lanes), lambda ph, i: (i, 0)),
                  pl.BlockSpec((tr, tr), lambda ph, i: (0, 0))],
        out_specs=(pl.BlockSpec((tr, lanes), lambda ph, i: (i * ph, 0)),
                   pl.BlockSpec((8, lanes), lambda ph, i: (0, 0))),
        scratch_shapes=[pltpu.VMEM((8, lanes), F32), pltpu.VMEM((8, lanes), F32)],
        compiler_params=_params(2),
        name="route",
    )(logits, tri)


def _dispatch_kernel(pos_ref, h2_hbm, xs_zero_hbm, xs_hbm, sem):
    del xs_zero_hbm
    ts = pos_ref.shape[1]
    base = pl.program_id(0) * ts

    def row_copy(t, k):
        return pltpu.make_async_copy(h2_hbm.at[pl.ds(base + t, 1)],
                                     xs_hbm.at[pl.ds(pos_ref[k, t], 1)], sem)

    def start(t, carry):
        row_copy(t, 0).start()
        row_copy(t, 1).start()
        return carry

    def wait(t, carry):
        row_copy(t, 0).wait()
        row_copy(t, 1).wait()
        return carry

    lax.fori_loop(0, ts, start, 0)
    lax.fori_loop(0, ts, wait, 0)


def _dispatch(pos_t, h2, n_rows):
    t, dh = h2.shape
    ts = min(ROW_TILE, t)
    xs_zero = jnp.zeros((n_rows, dh), h2.dtype)
    return pl.pallas_call(
        _dispatch_kernel,
        out_shape=jax.ShapeDtypeStruct((n_rows, dh), h2.dtype),
        grid=(t // ts,),
        in_specs=[pl.BlockSpec((2, ts), lambda i: (0, i), memory_space=pltpu.SMEM),
                  pl.BlockSpec(memory_space=pl.ANY),
                  pl.BlockSpec(memory_space=pl.ANY)],
        out_specs=pl.BlockSpec(memory_space=pl.ANY),
        scratch_shapes=[pltpu.SemaphoreType.DMA(())],
        input_output_aliases={2: 0},
        compiler_params=_params(1),
        name="dispatch",
    )(pos_t, h2, xs_zero)


def _experts_kernel(be_ref, nact_ref, xs_ref, wg_ref, wu_ref, wd_ref, ys_ref, wgb, wub, wdb):
    i = pl.program_id(0)
    dh = xs_ref.shape[1]

    @pl.when(i < nact_ref[0])
    def _():
        prev = be_ref[jnp.maximum(i - 1, 0)]

        @pl.when((i == 0) | (be_ref[i] != prev))
        def _():
            wgb[...] = wg_ref[0].astype(BF16)
            wub[...] = wu_ref[0].astype(BF16)
            wdb[...] = wd_ref[0].astype(BF16)

        xa, xb = _unpack_bf16_pair(xs_ref[...])
        xa, xb = xa.astype(BF16), xb.astype(BF16)
        g = (jnp.dot(xa, wgb[:dh], preferred_element_type=F32)
             + jnp.dot(xb, wgb[dh:], preferred_element_type=F32))
        u = (jnp.dot(xa, wub[:dh], preferred_element_type=F32)
             + jnp.dot(xb, wub[dh:], preferred_element_type=F32))
        a = (g * jax.nn.sigmoid(g) * u).astype(BF16)
        y = jnp.dot(a, wdb[...], preferred_element_type=F32)
        ys_ref[...] = _pack_bf16_pair(y[:, :dh], y[:, dh:])

    @pl.when(i >= nact_ref[0])
    def _():
        ys_ref[...] = jnp.zeros_like(ys_ref)


def _experts(block_e, n_active, xs, w_gate, w_up, w_down):
    n_rows, dh = xs.shape
    _, d, de = w_gate.shape
    n_blocks = n_rows // MOE_BLOCK
    grid_spec = pltpu.PrefetchScalarGridSpec(
        num_scalar_prefetch=2,
        grid=(n_blocks,),
        in_specs=[pl.BlockSpec((MOE_BLOCK, dh), lambda i, be, na: (i, 0)),
                  pl.BlockSpec((1, d, de), lambda i, be, na: (be[i], 0, 0)),
                  pl.BlockSpec((1, d, de), lambda i, be, na: (be[i], 0, 0)),
                  pl.BlockSpec((1, de, d), lambda i, be, na: (be[i], 0, 0))],
        out_specs=pl.BlockSpec((MOE_BLOCK, dh), lambda i, be, na: (i, 0)),
        scratch_shapes=[pltpu.VMEM((d, de), BF16), pltpu.VMEM((d, de), BF16),
                        pltpu.VMEM((de, d), BF16)])
    return pl.pallas_call(
        _experts_kernel,
        out_shape=jax.ShapeDtypeStruct((n_rows, dh), jnp.int32),
        grid_spec=grid_spec,
        compiler_params=_params(1),
        name="experts",
    )(block_e, n_active, xs, w_gate, w_up, w_down)


def _combine_kernel(pos_ref, x_ref, plan_ref, mod_ref, gfin_ref, ys_hbm, out_ref, buf, sem, *, final):
    ts, d = x_ref.shape[1], x_ref.shape[2]

    def row_copy(t, k):
        return pltpu.make_async_copy(ys_hbm.at[pl.ds(pos_ref[k, t], 1)],
                                     buf.at[k, pl.ds(t, 1)], sem)

    def start(t, carry):
        row_copy(t, 0).start()
        row_copy(t, 1).start()
        return carry

    def wait(t, carry):
        row_copy(t, 0).wait()
        row_copy(t, 1).wait()
        return carry

    lax.fori_loop(0, ts, start, 0)
    lax.fori_loop(0, ts, wait, 0)
    w1 = plan_ref[:, 2:3]
    w2 = plan_ref[:, 3:4]
    a1, b1 = _unpack_bf16_pair(buf[0])
    a2, b2 = _unpack_bf16_pair(buf[1])
    moe = jnp.concatenate([w1 * a1 + w2 * a2, w1 * b1 + w2 * b2], axis=1)
    x2 = x_ref[0] + mod_ref[0][5:6] * moe
    if final:
        ms = jnp.mean(x2 * x2, axis=-1, keepdims=True)
        x2 = x2 * lax.rsqrt(ms + NORM_EPS) * gfin_ref[...]
    out_ref[0] = x2


def _combine(pos_t, x1, plan, mod, g_final, ys, *, final):
    b, s, d = x1.shape
    ts = min(ROW_TILE, s)
    nst = s // ts
    return pl.pallas_call(
        functools.partial(_combine_kernel, final=final),
        out_shape=jax.ShapeDtypeStruct((b, s, d), F32),
        grid=(b, nst),
        in_specs=[pl.BlockSpec((2, ts), lambda bi, si: (0, bi * nst + si), memory_space=pltpu.SMEM),
                  pl.BlockSpec((1, ts, d), lambda bi, si: (bi, si, 0)),
                  pl.BlockSpec((ts, plan.shape[1]), lambda bi, si: (bi * nst + si, 0)),
                  pl.BlockSpec((1, 6, d), lambda bi, si: (bi, 0, 0)),
                  pl.BlockSpec((1, d), lambda bi, si: (0, 0)),
                  pl.BlockSpec(memory_space=pl.ANY)],
        out_specs=pl.BlockSpec((1, ts, d), lambda bi, si: (bi, si, 0)),
        scratch_shapes=[pltpu.VMEM((2, ts, d // 2), jnp.int32), pltpu.SemaphoreType.DMA(())],
        compiler_params=_params(2),
        name="combine_final" if final else "combine",
    )(pos_t, x1, plan, mod, g_final, ys)


def _hierarchical_moe(x1, h2, logits, mod, g_final, w_gate, w_up, w_down, *, final):
    t = h2.shape[0]
    plan, cnt = _route(logits)
    counts = cnt[0, :N_EXPERTS].astype(jnp.int32)
    n_blocks = -(-(t * 2) // MOE_BLOCK) + N_EXPERTS
    pends = jnp.cumsum((counts + MOE_BLOCK - 1) // MOE_BLOCK)
    block_e = jnp.minimum(jnp.searchsorted(pends, jnp.arange(n_blocks), side="right"),
                          N_EXPERTS - 1).astype(jnp.int32)
    n_active = pends[-1:].astype(jnp.int32)
    pos_t = plan[:, 0:2].astype(jnp.int32).T
    xs = _dispatch(pos_t, h2, n_blocks * MOE_BLOCK)
    ys = _experts(block_e, n_active, xs, w_gate, w_up, w_down)
    return _combine(pos_t, x1, plan, mod, g_final, ys, final=final)


def _qkv_kernel(x_ref, mod_ref, g_ref, w_ref, cos_ref, sa_ref, sb_ref, qkv_ref):
    x = x_ref[0]
    mod = mod_ref[0]
    h = _norm_mod(x, g_ref[...], mod[0:1], mod[1:2]).astype(BF16)
    wd = ATTN_OUT_DIM
    for slab in range(ATTN_IN_DIM // wd):
        acc = jnp.dot(h, w_ref[:, slab * wd:(slab + 1) * wd], preferred_element_type=F32)
        if slab % 3 != 2:
            acc = (acc * cos_ref[...] + pltpu.roll(acc, ROT_DIM // 2, 1) * sa_ref[...]
                   + pltpu.roll(acc, wd - ROT_DIM // 2, 1) * sb_ref[...])
        qkv_ref[0, :, slab * wd:(slab + 1) * wd] = acc.astype(BF16)


def _rope_tables(s):
    half = ROT_DIM // 2
    pos = jnp.arange(s, dtype=F32)
    inv_freq = jnp.power(jnp.float32(ROPE_THETA), -jnp.arange(0, ROT_DIM, 2, dtype=F32) / ROT_DIM)
    ang = pos[:, None] * inv_freq[None, :]
    cos, sin = jnp.cos(ang), jnp.sin(ang)
    m = jnp.arange(ATTN_OUT_DIM) % HEAD_DIM
    cos_l = jnp.where(m[None, :] < ROT_DIM, cos[:, m % half], 1.0)
    sa_l = jnp.where((m[None, :] >= half) & (m[None, :] < ROT_DIM), sin[:, m % half], 0.0)
    sb_l = jnp.where(m[None, :] < half, -sin[:, m % half], 0.0)
    return cos_l.astype(F32), sa_l.astype(F32), sb_l.astype(F32)


def _qkv_proj(x, mod, g_mix, w_in):
    b, s, d = x.shape
    ts = min(ROW_TILE, s)
    cos_l, sa_l, sb_l = _rope_tables(s)
    const = lambda si, bi: (0, 0)
    tab = pl.BlockSpec((ts, ATTN_OUT_DIM), lambda si, bi: (si, 0))
    return pl.pallas_call(
        _qkv_kernel,
        out_shape=jax.ShapeDtypeStruct((b, s, ATTN_IN_DIM), BF16),
        grid=(s // ts, b),
        in_specs=[pl.BlockSpec((1, ts, d), lambda si, bi: (bi, si, 0)),
                  pl.BlockSpec((1, 6, d), lambda si, bi: (bi, 0, 0)),
                  pl.BlockSpec((1, d), const),
                  pl.BlockSpec((d, ATTN_IN_DIM), const),
                  tab, tab, tab],
        out_specs=pl.BlockSpec((1, ts, ATTN_IN_DIM), lambda si, bi: (bi, si, 0)),
        compiler_params=_params(2),
        name="qkv_proj",
    )(x, mod, g_mix, w_in, cos_l, sa_l, sb_l)


def _attn_kernel(q_ref, kc_ref, kp_ref, vc_ref, vp_ref, o_ref, lse_ref):
    tq = q_ref.shape[1]
    sp = ATTN_SPAN
    first_tile = pl.program_id(2) == 0
    kext = jnp.concatenate([kp_ref[0], kc_ref[0]], axis=0)
    vext = jnp.concatenate([vp_ref[0], vc_ref[0]], axis=0)
    rq = lax.broadcasted_iota(jnp.int32, (sp, 2 * sp), 0)
    ck = lax.broadcasted_iota(jnp.int32, (sp, 2 * sp), 1)
    neg = jnp.float32(-jnp.inf)
    band = jnp.where(ck >= rq, jnp.where(ck <= rq + sp, 0.0, neg), neg)
    lane = lax.broadcasted_iota(jnp.int32, (sp, lse_ref.shape[2]), 1)
    scale = HEAD_DIM ** -0.5
    for i in range(tq // sp):
        if i == 0:
            bias = jnp.where(first_tile & (ck < sp), neg, band)
        else:
            bias = band
        qi = q_ref[0, i * sp:(i + 1) * sp, :]
        ki = kext[i * sp:(i + 2) * sp]
        vi = vext[i * sp:(i + 2) * sp]
        outs = []
        lse_tile = jnp.zeros((sp, lse_ref.shape[2]), F32)
        for h in range(ATTN_HEADS):
            hs = slice(h * HEAD_DIM, (h + 1) * HEAD_DIM)
            sc = lax.dot_general(qi[:, hs], ki[:, hs], (((1,), (1,)), ((), ())),
                                 preferred_element_type=F32) * scale + bias
            m = jnp.max(sc, axis=1, keepdims=True)
            p = jnp.exp(sc - m)
            den = jnp.sum(p, axis=1, keepdims=True)
            o = jnp.dot(p.astype(BF16), vi[:, hs], preferred_element_type=F32) / den
            outs.append(o)
            lse_tile = jnp.where(lane == h, m + jnp.log(den), lse_tile)
        o_ref[0, i * sp:(i + 1) * sp, :] = jnp.concatenate(outs, axis=1).astype(BF16)
        lse_ref[0, i * sp:(i + 1) * sp, :] = lse_tile


def _dilated_attention(qkv, g, dil):
    b, s, _ = qkv.shape
    L = s // dil
    tq = min(ATTN_Q_TILE, L)
    sub = tq // ATTN_SPAN
    wd = ATTN_OUT_DIM
    nslab = ATTN_IN_DIM // wd
    view = qkv.reshape(b, L, dil * ATTN_IN_DIM)

    def cur(j):
        return pl.BlockSpec((1, tq, wd), lambda bi, r, lb: (bi, lb, r * nslab + g * 3 + j))

    def prev(j):
        return pl.BlockSpec((1, ATTN_SPAN, wd),
                            lambda bi, r, lb: (bi, jnp.maximum(lb * sub - 1, 0), r * nslab + g * 3 + j))

    o, lse = pl.pallas_call(
        _attn_kernel,
        out_shape=(jax.ShapeDtypeStruct((b, L, dil * wd), BF16),
                   jax.ShapeDtypeStruct((b, L, dil * V7X_LANES), F32)),
        grid=(b, dil, L // tq),
        in_specs=[cur(0), cur(1), prev(1), cur(2), prev(2)],
        out_specs=(pl.BlockSpec((1, tq, wd), lambda bi, r, lb: (bi, lb, r)),
                   pl.BlockSpec((1, tq, V7X_LANES), lambda bi, r, lb: (bi, lb, r))),
        compiler_params=_params(3),
        name=f"dilated_attention_{dil}",
    )(view, view, view, view, view)
    return o.reshape(b * s, wd), lse.reshape(b * s, V7X_LANES)


def _attn_out_kernel(o0_ref, o1_ref, o2_ref, l0_ref, l1_ref, l2_ref, x_ref, mod_ref, wout_ref,
                     g2_ref, wr_ref, br_ref, x1_ref, h2_ref, lg_ref):
    ts = x_ref.shape[1]
    l0, l1, l2 = l0_ref[...], l1_ref[...], l2_ref[...]
    m = jnp.maximum(jnp.maximum(l0, l1), l2)
    e0, e1, e2 = jnp.exp(l0 - m), jnp.exp(l1 - m), jnp.exp(l2 - m)
    den = e0 + e1 + e2
    head = lax.broadcasted_iota(jnp.int32, (ts, ATTN_OUT_DIM), 1) // HEAD_DIM
    merged = jnp.zeros((ts, ATTN_OUT_DIM), F32)
    for e, o_ref in ((e0, o0_ref), (e1, o1_ref), (e2, o2_ref)):
        w = e / den
        wfull = jnp.zeros((ts, ATTN_OUT_DIM), F32)
        for h in range(ATTN_HEADS):
            wfull = jnp.where(head == h, w[:, h:h + 1], wfull)
        merged = merged + wfull * o_ref[...].astype(F32)
    y = jnp.dot(merged.astype(BF16), wout_ref[...], preferred_element_type=F32)
    mod = mod_ref[0]
    x1 = x_ref[0] + mod[2:3] * y
    x1_ref[0] = x1
    _ffn_input_and_logits(x1, mod, g2_ref, wr_ref, br_ref, h2_ref, lg_ref)


def _attn_out(outs, lses, x, mod, w_out, g_ffn, wr, br):
    b, s, d = x.shape
    ts = min(ROW_TILE, s)
    nst = s // ts
    const = lambda bi, si: (0, 0)
    tok = lambda bi, si: (bi * nst + si, 0)
    return pl.pallas_call(
        _attn_out_kernel,
        out_shape=(jax.ShapeDtypeStruct((b, s, d), F32),
                   jax.ShapeDtypeStruct((b * s, d // 2), jnp.int32),
                   jax.ShapeDtypeStruct((b * s, V7X_LANES), F32)),
        grid=(b, nst),
        in_specs=[pl.BlockSpec((ts, ATTN_OUT_DIM), tok)] * 3
                 + [pl.BlockSpec((ts, V7X_LANES), tok)] * 3
                 + [pl.BlockSpec((1, ts, d), lambda bi, si: (bi, si, 0)),
                    pl.BlockSpec((1, 6, d), lambda bi, si: (bi, 0, 0)),
                    pl.BlockSpec((ATTN_OUT_DIM, d), const),
                    pl.BlockSpec((1, d), const),
                    pl.BlockSpec((d, V7X_LANES), const),
                    pl.BlockSpec((1, V7X_LANES), const)],
        out_specs=(pl.BlockSpec((1, ts, d), lambda bi, si: (bi, si, 0)),
                   pl.BlockSpec((ts, d // 2), tok),
                   pl.BlockSpec((ts, V7X_LANES), tok)),
        compiler_params=_params(2),
        name="attn_out",
    )(*outs, *lses, x, mod, w_out, g_ffn, wr, br)


def _router_params(w_grp, b_grp, w_exp, b_exp):
    d = w_grp.shape[0]
    w_e = jnp.transpose(w_exp, (1, 0, 2)).reshape(d, N_EXPERTS)
    pad1 = ROUTER_LANE_OFFSET - N_GROUPS
    pad2 = V7X_LANES - ROUTER_LANE_OFFSET - N_EXPERTS
    wr = jnp.concatenate([w_grp, jnp.zeros((d, pad1), F32), w_e, jnp.zeros((d, pad2), F32)], axis=1)
    br = jnp.concatenate([b_grp, jnp.zeros((pad1,), F32), b_exp.reshape(-1), jnp.zeros((pad2,), F32)])
    return wr.astype(BF16), br.reshape(1, V7X_LANES).astype(F32)


def kernel(x, c, norm_mix_g, norm_ffn_g, ada_w, ada_b, conv_in_w, conv_w, conv_out_w, attn_in_w,
           attn_out_w, router_grp_w, router_grp_b, router_exp_w, router_exp_b, exp_gate_w, exp_up_w,
           exp_down_w, final_norm_g):
    b, s, d = x.shape
    depth = ada_w.shape[0]
    assert depth == 2 and s % (DILATED_GROUPS[-1][1] * ATTN_SPAN) == 0
    mod = _ada_modulation(c, ada_w, ada_b).reshape(depth, b, 6, d)
    g_fin = final_norm_g.reshape(1, d)

    wr, br = _router_params(router_grp_w[0], router_grp_b[0], router_exp_w[0], router_exp_b[0])
    x1, h2, logits = _conv_mixer(x, mod[0], norm_mix_g[0:1], conv_in_w[0].astype(BF16), conv_w[0],
                                 conv_out_w[0].astype(BF16), norm_ffn_g[0:1], wr, br)
    x2 = _hierarchical_moe(x1, h2, logits, mod[0], g_fin, exp_gate_w[0], exp_up_w[0], exp_down_w[0],
                           final=False)

    wr, br = _router_params(router_grp_w[1], router_grp_b[1], router_exp_w[1], router_exp_b[1])
    qkv = _qkv_proj(x2, mod[1], norm_mix_g[1:2], attn_in_w[0].astype(BF16))
    outs, lses = zip(*[_dilated_attention(qkv, g, dil) for g, (_, dil) in enumerate(DILATED_GROUPS)])
    x3, h2, logits = _attn_out(outs, lses, x2, mod[1], attn_out_w[0].astype(BF16), norm_ffn_g[1:2],
                               wr, br)
    return _hierarchical_moe(x3, h2, logits, mod[1], g_fin, exp_gate_w[1], exp_up_w[1], exp_down_w[1],
                             final=True)
```

```python
import functools

import jax
import jax.numpy as jnp
from jax import lax
from jax.experimental import pallas as pl
from jax.experimental.pallas import tpu as pltpu

CONV_WIDTH = 3
DILATED_GROUPS = ((128, 1), (512, 4), (2048, 16))
N_DIL_GROUPS = len(DILATED_GROUPS)
ATTN_HEADS = 8
HEAD_DIM = 64
ATTN_OUT_DIM = ATTN_HEADS * HEAD_DIM
ATTN_IN_DIM = N_DIL_GROUPS * 3 * ATTN_OUT_DIM
ROT_DIM = HEAD_DIM // 4
ROPE_THETA = 500000.0
N_GROUPS = 4
EXPERTS_PER_GROUP = 8
N_EXPERTS = N_GROUPS * EXPERTS_PER_GROUP
NORM_EPS = 1e-6
ATTN_SPAN = 128

V7X_LANES = 128
V7X_VMEM_LIMIT_BYTES = 56 * 1024 * 1024

ROW_TILE = 512
CONV_COL_BLOCK = 256
ROUTE_TILE = 1024
MOE_BLOCK = 512
ATTN_Q_TILE = 256
ROUTER_LANE_OFFSET = 8
DMA_ISSUE_UNROLL = 8

F32 = jnp.float32
BF16 = jnp.bfloat16


def _params(n_axes):
    return pltpu.CompilerParams(dimension_semantics=("arbitrary",) * n_axes,
                                vmem_limit_bytes=V7X_VMEM_LIMIT_BYTES)


def _norm_mod(x, g, shift, scale):
    ms = jnp.mean(x * x, axis=-1, keepdims=True)
    y = x * lax.rsqrt(ms + NORM_EPS) * g
    return y * (1.0 + scale) + shift


def _pack_bf16_pair(a, b):
    return pltpu.pack_elementwise([a, b], packed_dtype=BF16)


def _unpack_bf16_pair(w):
    a = pltpu.unpack_elementwise(w, index=0, packed_dtype=BF16, unpacked_dtype=F32)
    b = pltpu.unpack_elementwise(w, index=1, packed_dtype=BF16, unpacked_dtype=F32)
    return a, b


def _store_row_groups(ref, x):
    rows, d = x.shape
    nj = d // (2 * V7X_LANES)
    for j in range(nj):
        lo = x[:, j * V7X_LANES:(j + 1) * V7X_LANES]
        hi = x[:, d // 2 + j * V7X_LANES:d // 2 + (j + 1) * V7X_LANES]
        ref[pl.ds(j, rows, stride=nj), :] = _pack_bf16_pair(lo, hi)


def _load_row_groups(ref, rows, nj):
    los, his = [], []
    for j in range(nj):
        lo, hi = _unpack_bf16_pair(ref[pl.ds(j, rows, stride=nj), :])
        los.append(lo)
        his.append(hi)
    return los + his


def _ffn_input_and_logits(x1, mod, g2_ref, wr_ref, br_ref, h2_ref, lg_ref):
    h2 = _norm_mod(x1, g2_ref[...], mod[3:4], mod[4:5])
    _store_row_groups(h2_ref, h2)
    lg_ref[...] = jnp.dot(h2.astype(BF16), wr_ref[...], preferred_element_type=F32) + br_ref[...]


def _ada_kernel(c_ref, w_ref, b_ref, o_ref):
    c = c_ref[...]
    ca = c * jax.nn.sigmoid(c)
    o_ref[0] = jnp.dot(ca, w_ref[0], preferred_element_type=F32,
                       precision=lax.Precision.HIGHEST) + b_ref[0]


def _ada_modulation(c, ada_w, ada_b):
    depth, d, n = ada_w.shape
    b = c.shape[0]
    tn = d
    return pl.pallas_call(
        _ada_kernel,
        out_shape=jax.ShapeDtypeStruct((depth, b, n), F32),
        grid=(depth, n // tn),
        in_specs=[pl.BlockSpec((b, d), lambda i, j: (0, 0)),
                  pl.BlockSpec((1, d, tn), lambda i, j: (i, 0, j)),
                  pl.BlockSpec((1, 1, tn), lambda i, j: (i, 0, j))],
        out_specs=pl.BlockSpec((1, b, tn), lambda i, j: (i, 0, j)),
        compiler_params=_params(2),
        name="ada_modulation",
    )(c, ada_w, ada_b.reshape(depth, 1, n))


def _conv_mixer_kernel(x_ref, mod_ref, g_ref, win_ref, cw_ref, wout_ref, g2_ref, wr_ref, br_ref,
                       x1_ref, h2_ref, lg_ref, carry_ref, v_ref, *, cb):
    ts, d = x_ref.shape[1], x_ref.shape[2]

    @pl.when(pl.program_id(1) == 0)
    def _():
        carry_ref[...] = jnp.zeros_like(carry_ref)

    x = x_ref[0]
    mod = mod_ref[0]
    h = _norm_mod(x, g_ref[...], mod[0:1], mod[1:2]).astype(BF16)
    row = lax.broadcasted_iota(jnp.int32, (ts, cb), 0)
    for j in range(d // cb):
        lo = j * cb
        bj = jnp.dot(h, win_ref[:, lo:lo + cb], preferred_element_type=F32)
        cj = jnp.dot(h, win_ref[:, d + lo:d + lo + cb], preferred_element_type=F32)
        uj = jnp.dot(h, win_ref[:, 2 * d + lo:2 * d + lo + cb], preferred_element_type=F32)
        z = cj * uj
        prev = carry_ref[:, lo:lo + cb]
        z1 = jnp.where(row == 0, prev[7:8], pltpu.roll(z, 1, 0))
        z2 = jnp.where(row == 0, prev[6:7], jnp.where(row == 1, prev[7:8], pltpu.roll(z, 2, 0)))
        zc = cw_ref[0:1, lo:lo + cb] * z2 + cw_ref[1:2, lo:lo + cb] * z1 + cw_ref[2:3, lo:lo + cb] * z
        carry_ref[:, lo:lo + cb] = z[ts - 8:ts]
        v_ref[:, lo:lo + cb] = (bj * zc).astype(BF16)
    y = jnp.dot(v_ref[...], wout_ref[...], preferred_element_type=F32)
    x1 = x + mod[2:3] * y
    x1_ref[0] = x1
    _ffn_input_and_logits(x1, mod, g2_ref, wr_ref, br_ref, h2_ref, lg_ref)


def _conv_mixer(x, mod, g_mix, w_in, conv_w, w_out, g_ffn, wr, br):
    b, s, d = x.shape
    ts = min(ROW_TILE, s)
    cb = min(CONV_COL_BLOCK, d)
    nst = s // ts
    nj = d // (2 * V7X_LANES)
    const = lambda bi, si: (0, 0)
    return pl.pallas_call(
        functools.partial(_conv_mixer_kernel, cb=cb),
        out_shape=(jax.ShapeDtypeStruct((b, s, d), F32),
                   jax.ShapeDtypeStruct((b * s * nj, V7X_LANES), jnp.int32),
                   jax.ShapeDtypeStruct((b * s, V7X_LANES), F32)),
        grid=(b, nst),
        in_specs=[pl.BlockSpec((1, ts, d), lambda bi, si: (bi, si, 0)),
                  pl.BlockSpec((1, 6, d), lambda bi, si: (bi, 0, 0)),
                  pl.BlockSpec((1, d), const),
                  pl.BlockSpec((d, 3 * d), const),
                  pl.BlockSpec((CONV_WIDTH, d), const),
                  pl.BlockSpec((d, d), const),
                  pl.BlockSpec((1, d), const),
                  pl.BlockSpec((d, V7X_LANES), const),
                  pl.BlockSpec((1, V7X_LANES), const)],
        out_specs=(pl.BlockSpec((1, ts, d), lambda bi, si: (bi, si, 0)),
                   pl.BlockSpec((ts * nj, V7X_LANES), lambda bi, si: (bi * nst + si, 0)),
                   pl.BlockSpec((ts, V7X_LANES), lambda bi, si: (bi * nst + si, 0))),
        scratch_shapes=[pltpu.VMEM((8, d), F32), pltpu.VMEM((ts, d), BF16)],
        compiler_params=_params(2),
        name="conv_mixer",
    )(x, mod, g_mix, w_in, conv_w, w_out, g_ffn, wr, br)


def _route_kernel(lg_ref, tri_ref, plan_ref, cnt_ref, run_ref, pst_ref, *, blk):
    ph, i = pl.program_id(0), pl.program_id(1)
    tr = lg_ref.shape[0]
    lanes = lg_ref.shape[1]

    @pl.when((ph == 0) & (i == 0))
    def _():
        run_ref[...] = jnp.zeros_like(run_ref)

    lg = lg_ref[...]
    lane = lax.broadcasted_iota(jnp.int32, (tr, lanes), 1)
    big = jnp.int32(lanes)
    neg = jnp.float32(-jnp.inf)
    gmask = lane < N_GROUPS
    gl = jnp.where(gmask, lg, neg)
    mg = jnp.max(gl, axis=1, keepdims=True)
    gidx = jnp.min(jnp.where(gl == mg, lane, big), axis=1, keepdims=True)
    den_g = jnp.sum(jnp.where(gmask, jnp.exp(lg - mg), 0.0), axis=1, keepdims=True)
    grp_p = 1.0 / den_g
    lo = ROUTER_LANE_OFFSET + EXPERTS_PER_GROUP * gidx
    el = jnp.where((lane >= lo) & (lane < lo + EXPERTS_PER_GROUP), lg, neg)
    l1 = jnp.max(el, axis=1, keepdims=True)
    i1 = jnp.min(jnp.where(el == l1, lane, big), axis=1, keepdims=True)
    el2 = jnp.where(lane == i1, neg, el)
    l2 = jnp.max(el2, axis=1, keepdims=True)
    i2 = jnp.min(jnp.where(el2 == l2, lane, big), axis=1, keepdims=True)
    r = jnp.exp(l2 - l1)
    w1 = grp_p / (1.0 + r)
    w2 = grp_p * r / (1.0 + r)
    oh1 = (lane == i1 - ROUTER_LANE_OFFSET).astype(F32)
    oh2 = (lane == i2 - ROUTER_LANE_OFFSET).astype(F32)
    tot1 = jnp.sum(oh1, axis=0, keepdims=True)
    tot2 = jnp.sum(oh2, axis=0, keepdims=True)
    run = run_ref[0:1, :]

    @pl.when(ph == 0)
    def _():
        new_run = run + tot1 + tot2
        run_ref[...] = jnp.broadcast_to(new_run, run_ref.shape)

        @pl.when(i == pl.num_programs(1) - 1)
        def _():
            cnt = jnp.broadcast_to(new_run, run_ref.shape)
            cnt_ref[...] = cnt
            padded = jnp.floor((cnt + (blk - 1)) / blk) * blk
            l8 = lax.broadcasted_iota(jnp.int32, cnt.shape, 1)
            csum = padded
            k = 1
            while k < N_EXPERTS:
                csum = csum + jnp.where(l8 >= k, pltpu.roll(csum, k, 1), 0.0)
                k *= 2
            pst_ref[...] = csum - padded
            run_ref[...] = jnp.zeros_like(run_ref)

    @pl.when(ph == 1)
    def _():
        oh = jnp.concatenate([oh1, oh2], axis=1).astype(BF16)
        before = jnp.dot(tri_ref[...], oh, preferred_element_type=F32)
        base = pst_ref[0:1, :] + run
        pos1 = jnp.sum(oh1 * (base + before[:, :lanes]), axis=1, keepdims=True)
        pos2 = jnp.sum(oh2 * (base + tot1 + before[:, lanes:]), axis=1, keepdims=True)
        run_ref[...] = jnp.broadcast_to(run + tot1 + tot2, run_ref.shape)
        plan_ref[...] = jnp.where(lane == 0, pos1, jnp.where(lane == 1, pos2,
                                  jnp.where(lane == 2, w1, jnp.where(lane == 3, w2, 0.0))))


def _route(logits):
    t, lanes = logits.shape
    tr = min(ROUTE_TILE, t)
    nt = t // tr
    tri = (jnp.arange(tr)[None, :] < jnp.arange(tr)[:, None]).astype(BF16)
    return pl.pallas_call(
        functools.partial(_route_kernel, blk=MOE_BLOCK),
        out_shape=(jax.ShapeDtypeStruct((t, lanes), F32),
                   jax.ShapeDtypeStruct((8, lanes), F32)),
        grid=(2, nt),
        in_specs=[pl.BlockSpec((tr, lanes), lambda ph, i: (i, 0)),
                  pl.BlockSpec((tr, tr), lambda ph, i: (0, 0))],
        out_specs=(pl.BlockSpec((tr, lanes), lambda ph, i: (i * ph, 0)),
                   pl.BlockSpec((8, lanes), lambda ph, i: (0, 0))),
        scratch_shapes=[pltpu.VMEM((8, lanes), F32), pltpu.VMEM((8, lanes), F32)],
        compiler_params=_params(2),
        name="route",
    )(logits, tri)


def _dispatch_kernel(row_ref, h2_ref, xs_zero_hbm, xs_hbm, sem, *, nj):
    del xs_zero_hbm
    ts = row_ref.shape[1]

    def start(t, carry):
        src = h2_ref.at[pl.ds(pl.multiple_of(t * nj, nj), nj)]
        pltpu.make_async_copy(src, xs_hbm.at[pl.ds(pl.multiple_of(row_ref[0, t], nj), nj)], sem).start()
        pltpu.make_async_copy(src, xs_hbm.at[pl.ds(pl.multiple_of(row_ref[1, t], nj), nj)], sem).start()
        return carry

    lax.fori_loop(0, ts, start, 0, unroll=DMA_ISSUE_UNROLL)
    for _ in range(2):
        pltpu.make_async_copy(h2_ref, xs_hbm.at[pl.ds(0, ts * nj)], sem).wait()


def _dispatch(row_t, h2, n_rows, nj):
    t = row_t.shape[1]
    ts = min(ROW_TILE, t)
    xs_zero = jnp.zeros((n_rows * nj, V7X_LANES), h2.dtype)
    return pl.pallas_call(
        functools.partial(_dispatch_kernel, nj=nj),
        out_shape=jax.ShapeDtypeStruct((n_rows * nj, V7X_LANES), h2.dtype),
        grid=(t // ts,),
        in_specs=[pl.BlockSpec((2, ts), lambda i: (0, i), memory_space=pltpu.SMEM),
                  pl.BlockSpec((ts * nj, V7X_LANES), lambda i: (i, 0)),
                  pl.BlockSpec(memory_space=pl.ANY)],
        out_specs=pl.BlockSpec(memory_space=pl.ANY),
        scratch_shapes=[pltpu.SemaphoreType.DMA(())],
        input_output_aliases={2: 0},
        compiler_params=_params(1),
        name="dispatch",
    )(row_t, h2, xs_zero)


def _experts_kernel(be_ref, nact_ref, xs_ref, wg_ref, wu_ref, wd_ref, ys_ref, wgb, wub, wdb, *, nj):
    i = pl.program_id(0)
    blk = xs_ref.shape[0] // nj

    @pl.when(i < nact_ref[0])
    def _():
        prev = be_ref[jnp.maximum(i - 1, 0)]

        @pl.when((i == 0) | (be_ref[i] != prev))
        def _():
            wgb[...] = wg_ref[0].astype(BF16)
            wub[...] = wu_ref[0].astype(BF16)
            wdb[...] = wd_ref[0].astype(BF16)

        x = jnp.concatenate([p.astype(BF16) for p in _load_row_groups(xs_ref, blk, nj)], axis=1)
        g = jnp.dot(x, wgb[...], preferred_element_type=F32)
        u = jnp.dot(x, wub[...], preferred_element_type=F32)
        a = (g * jax.nn.sigmoid(g) * u).astype(BF16)
        y = jnp.dot(a, wdb[...], preferred_element_type=F32)
        _store_row_groups(ys_ref, y)

    @pl.when(i >= nact_ref[0])
    def _():
        ys_ref[...] = jnp.zeros_like(ys_ref)


def _experts(block_e, n_active, xs, w_gate, w_up, w_down):
    _, d, de = w_gate.shape
    nj = d // (2 * V7X_LANES)
    n_blocks = xs.shape[0] // (MOE_BLOCK * nj)
    grid_spec = pltpu.PrefetchScalarGridSpec(
        num_scalar_prefetch=2,
        grid=(n_blocks,),
        in_specs=[pl.BlockSpec((MOE_BLOCK * nj, V7X_LANES), lambda i, be, na: (i, 0)),
                  pl.BlockSpec((1, d, de), lambda i, be, na: (be[i], 0, 0)),
                  pl.BlockSpec((1, d, de), lambda i, be, na: (be[i], 0, 0)),
                  pl.BlockSpec((1, de, d), lambda i, be, na: (be[i], 0, 0))],
        out_specs=pl.BlockSpec((MOE_BLOCK * nj, V7X_LANES), lambda i, be, na: (i, 0)),
        scratch_shapes=[pltpu.VMEM((d, de), BF16), pltpu.VMEM((d, de), BF16),
                        pltpu.VMEM((de, d), BF16)])
    return pl.pallas_call(
        functools.partial(_experts_kernel, nj=nj),
        out_shape=jax.ShapeDtypeStruct(xs.shape, jnp.int32),
        grid_spec=grid_spec,
        compiler_params=_params(1),
        name="experts",
    )(block_e, n_active, xs, w_gate, w_up, w_down)


def _combine_kernel(row_ref, x_ref, plan_ref, mod_ref, gfin_ref, ys_hbm, out_ref, buf, sem, *, final, nj):
    ts, d = x_ref.shape[1], x_ref.shape[2]

    def start(t, carry):
        dst = pl.ds(pl.multiple_of(t * nj, nj), nj)
        for k in range(2):
            pltpu.make_async_copy(ys_hbm.at[pl.ds(pl.multiple_of(row_ref[k, t], nj), nj)],
                                  buf.at[k, dst], sem).start()
        return carry

    lax.fori_loop(0, ts, start, 0, unroll=DMA_ISSUE_UNROLL)
    for k in range(2):
        pltpu.make_async_copy(ys_hbm.at[pl.ds(0, ts * nj)], buf.at[k], sem).wait()
    w1 = plan_ref[:, 2:3]
    w2 = plan_ref[:, 3:4]
    p1 = _load_row_groups(buf.at[0], ts, nj)
    p2 = _load_row_groups(buf.at[1], ts, nj)
    moe = jnp.concatenate([w1 * a + w2 * b for a, b in zip(p1, p2)], axis=1)
    x2 = x_ref[0] + mod_ref[0][5:6] * moe
    if final:
        ms = jnp.mean(x2 * x2, axis=-1, keepdims=True)
        x2 = x2 * lax.rsqrt(ms + NORM_EPS) * gfin_ref[...]
    out_ref[0] = x2


def _combine(row_t, x1, plan, mod, g_final, ys, *, final):
    b, s, d = x1.shape
    ts = min(ROW_TILE, s)
    nst = s // ts
    nj = d // (2 * V7X_LANES)
    return pl.pallas_call(
        functools.partial(_combine_kernel, final=final, nj=nj),
        out_shape=jax.ShapeDtypeStruct((b, s, d), F32),
        grid=(b, nst),
        in_specs=[pl.BlockSpec((2, ts), lambda bi, si: (0, bi * nst + si), memory_space=pltpu.SMEM),
                  pl.BlockSpec((1, ts, d), lambda bi, si: (bi, si, 0)),
                  pl.BlockSpec((ts, plan.shape[1]), lambda bi, si: (bi * nst + si, 0)),
                  pl.BlockSpec((1, 6, d), lambda bi, si: (bi, 0, 0)),
                  pl.BlockSpec((1, d), lambda bi, si: (0, 0)),
                  pl.BlockSpec(memory_space=pl.ANY)],
        out_specs=pl.BlockSpec((1, ts, d), lambda bi, si: (bi, si, 0)),
        scratch_shapes=[pltpu.VMEM((2, ts * nj, V7X_LANES), jnp.int32), pltpu.SemaphoreType.DMA(())],
        compiler_params=_params(2),
        name="combine_final" if final else "combine",
    )(row_t, x1, plan, mod, g_final, ys)


def _hierarchical_moe(x1, h2, logits, mod, g_final, w_gate, w_up, w_down, *, final):
    t = logits.shape[0]
    nj = h2.shape[0] // t
    plan, cnt = _route(logits)
    counts = cnt[0, :N_EXPERTS].astype(jnp.int32)
    n_blocks = -(-(t * 2) // MOE_BLOCK) + N_EXPERTS
    pends = jnp.cumsum((counts + MOE_BLOCK - 1) // MOE_BLOCK)
    block_e = jnp.minimum(jnp.sum(pends[None, :] <= jnp.arange(n_blocks)[:, None], axis=1),
                          N_EXPERTS - 1).astype(jnp.int32)
    n_active = pends[-1:].astype(jnp.int32)
    row_t = (plan[:, 0:2].astype(jnp.int32) * nj).T
    xs = _dispatch(row_t, h2, n_blocks * MOE_BLOCK, nj)
    ys = _experts(block_e, n_active, xs, w_gate, w_up, w_down)
    return _combine(row_t, x1, plan, mod, g_final, ys, final=final)


def _qkv_kernel(x_ref, mod_ref, g_ref, w_ref, cos_ref, sa_ref, sb_ref, o0_ref, o1_ref, o2_ref, scr):
    ts = x_ref.shape[1]
    x = x_ref[0]
    mod = mod_ref[0]
    h = _norm_mod(x, g_ref[...], mod[0:1], mod[1:2]).astype(BF16)
    wd = ATTN_OUT_DIM
    for g, o_ref in enumerate((o0_ref, o1_ref, o2_ref)):
        dil = DILATED_GROUPS[g][1]
        for j in range(3):
            slab = g * 3 + j
            acc = jnp.dot(h, w_ref[:, slab * wd:(slab + 1) * wd], preferred_element_type=F32)
            if j != 2:
                acc = (acc * cos_ref[...] + pltpu.roll(acc, ROT_DIM // 2, 1) * sa_ref[...]
                       + pltpu.roll(acc, wd - ROT_DIM // 2, 1) * sb_ref[...])
            if dil == 1:
                o_ref[0, 0, :, j * wd:(j + 1) * wd] = acc.astype(BF16)
            else:
                nc = wd // V7X_LANES
                for cc in range(nc):
                    scr[cc] = acc[:, cc * V7X_LANES:(cc + 1) * V7X_LANES]
                for r in range(dil):
                    rows = pl.ds(r, ts // dil, stride=dil)
                    o_ref[0, r, :, j * wd:(j + 1) * wd] = jnp.concatenate(
                        [scr[cc, rows, :] for cc in range(nc)], axis=1).astype(BF16)


def _rope_tables(s):
    half = ROT_DIM // 2
    pos = jnp.arange(s, dtype=F32)
    inv_freq = jnp.power(jnp.float32(ROPE_THETA), -jnp.arange(0, ROT_DIM, 2, dtype=F32) / ROT_DIM)
    ang = pos[:, None] * inv_freq[None, :]
    cos, sin = jnp.cos(ang), jnp.sin(ang)
    m = jnp.arange(ATTN_OUT_DIM) % HEAD_DIM
    cos_l = jnp.where(m[None, :] < ROT_DIM, cos[:, m % half], 1.0)
    sa_l = jnp.where((m[None, :] >= half) & (m[None, :] < ROT_DIM), sin[:, m % half], 0.0)
    sb_l = jnp.where(m[None, :] < half, -sin[:, m % half], 0.0)
    return cos_l.astype(F32), sa_l.astype(F32), sb_l.astype(F32)


def _qkv_proj(x, mod, g_mix, w_in):
    b, s, d = x.shape
    ts = min(ROW_TILE, s)
    cos_l, sa_l, sb_l = _rope_tables(s)
    const = lambda si, bi: (0, 0)
    tab = pl.BlockSpec((ts, ATTN_OUT_DIM), lambda si, bi: (si, 0))
    wd3 = 3 * ATTN_OUT_DIM
    return pl.pallas_call(
        _qkv_kernel,
        out_shape=tuple(jax.ShapeDtypeStruct((b, dil, s // dil, wd3), BF16) for _, dil in DILATED_GROUPS),
        grid=(s // ts, b),
        in_specs=[pl.BlockSpec((1, ts, d), lambda si, bi: (bi, si, 0)),
                  pl.BlockSpec((1, 6, d), lambda si, bi: (bi, 0, 0)),
                  pl.BlockSpec((1, d), const),
                  pl.BlockSpec((d, ATTN_IN_DIM), const),
                  tab, tab, tab],
        out_specs=tuple(pl.BlockSpec((1, dil, ts // dil, wd3), lambda si, bi: (bi, 0, si, 0))
                        for _, dil in DILATED_GROUPS),
        scratch_shapes=[pltpu.VMEM((ATTN_OUT_DIM // V7X_LANES, ts, V7X_LANES), F32)],
        compiler_params=_params(2),
        name="qkv_proj",
    )(x, mod, g_mix, w_in, cos_l, sa_l, sb_l)


def _attn_kernel(q_ref, kc_ref, kp_ref, vc_ref, vp_ref, o_ref, lse_ref):
    tq = q_ref.shape[2]
    sp = ATTN_SPAN
    nl = lse_ref.shape[3]
    first_tile = pl.program_id(2) == 0
    kext = jnp.concatenate([kp_ref[0, 0], kc_ref[0, 0]], axis=0)
    vext = jnp.concatenate([vp_ref[0, 0], vc_ref[0, 0]], axis=0)
    rq = lax.broadcasted_iota(jnp.int32, (sp, 2 * sp), 0)
    ck = lax.broadcasted_iota(jnp.int32, (sp, 2 * sp), 1)
    neg = jnp.float32(-jnp.inf)
    band = jnp.where(ck >= rq, jnp.where(ck <= rq + sp, 0.0, neg), neg)
    lane = lax.broadcasted_iota(jnp.int32, (sp, nl), 1)
    scale = HEAD_DIM ** -0.5
    for i in range(tq // sp):
        if i == 0:
            bias = jnp.where(first_tile & (ck < sp), neg, band)
        else:
            bias = band
        qi = q_ref[0, 0, i * sp:(i + 1) * sp, :]
        ki = kext[i * sp:(i + 2) * sp]
        vi = vext[i * sp:(i + 2) * sp]
        outs = []
        lse_tile = jnp.zeros((sp, nl), F32)
        for h in range(ATTN_HEADS):
            hs = slice(h * HEAD_DIM, (h + 1) * HEAD_DIM)
            sc = lax.dot_general(qi[:, hs], ki[:, hs], (((1,), (1,)), ((), ())),
                                 preferred_element_type=F32) * scale + bias
            m = jnp.max(sc, axis=1, keepdims=True)
            p = jnp.exp(sc - m)
            den = jnp.sum(p, axis=1, keepdims=True)
            o = jnp.dot(p.astype(BF16), vi[:, hs], preferred_element_type=F32) / den
            outs.append(o)
            lse_tile = jnp.where(lane == h, m + jnp.log(den), lse_tile)
        o_ref[0, 0, i * sp:(i + 1) * sp, :] = jnp.concatenate(outs, axis=1).astype(BF16)
        lse_ref[0, 0, i * sp:(i + 1) * sp, :] = lse_tile


def _dilated_attention(qkv_g):
    b, dil, L, _ = qkv_g.shape
    tq = min(ATTN_Q_TILE, L)
    sub = tq // ATTN_SPAN
    wd = ATTN_OUT_DIM

    def cur(j):
        return pl.BlockSpec((1, 1, tq, wd), lambda bi, r, lb: (bi, r, lb, j))

    def prev(j):
        return pl.BlockSpec((1, 1, ATTN_SPAN, wd),
                            lambda bi, r, lb: (bi, r, jnp.maximum(lb * sub - 1, 0), j))

    return pl.pallas_call(
        _attn_kernel,
        out_shape=(jax.ShapeDtypeStruct((b, dil, L, wd), BF16),
                   jax.ShapeDtypeStruct((b, dil, L, V7X_LANES), F32)),
        grid=(b, dil, L // tq),
        in_specs=[cur(0), cur(1), prev(1), cur(2), prev(2)],
        out_specs=(pl.BlockSpec((1, 1, tq, wd), lambda bi, r, lb: (bi, r, lb, 0)),
                   pl.BlockSpec((1, 1, tq, V7X_LANES), lambda bi, r, lb: (bi, r, lb, 0))),
        compiler_params=_params(3),
        name=f"dilated_attention_{dil}",
    )(qkv_g, qkv_g, qkv_g, qkv_g, qkv_g)


def _attn_out_kernel(o0_ref, o1_ref, o2_ref, l0_ref, l1_ref, l2_ref, x_ref, mod_ref, wout_ref,
                     g2_ref, wr_ref, br_ref, x1_ref, h2_ref, lg_ref, o_scr, l_scr):
    ts = x_ref.shape[1]
    nc = ATTN_OUT_DIM // V7X_LANES
    for g, (o_ref, l_ref) in enumerate(((o0_ref, l0_ref), (o1_ref, l1_ref), (o2_ref, l2_ref))):
        dil = DILATED_GROUPS[g][1]
        for r in range(dil):
            rows = pl.ds(r, ts // dil, stride=dil) if dil > 1 else slice(None)
            o_r = o_ref[0, r].astype(F32)
            for cc in range(nc):
                o_scr[g, cc, rows, :] = o_r[:, cc * V7X_LANES:(cc + 1) * V7X_LANES]
            l_scr[g, rows, :] = l_ref[0, r]
    l0, l1, l2 = l_scr[0], l_scr[1], l_scr[2]
    m = jnp.maximum(jnp.maximum(l0, l1), l2)
    es = (jnp.exp(l0 - m), jnp.exp(l1 - m), jnp.exp(l2 - m))
    den = es[0] + es[1] + es[2]
    head = lax.broadcasted_iota(jnp.int32, (ts, ATTN_OUT_DIM), 1) // HEAD_DIM
    merged = jnp.zeros((ts, ATTN_OUT_DIM), F32)
    for g in range(N_DIL_GROUPS):
        w = es[g] / den
        wfull = jnp.zeros((ts, ATTN_OUT_DIM), F32)
        for h in range(ATTN_HEADS):
            wfull = jnp.where(head == h, w[:, h:h + 1], wfull)
        merged = merged + wfull * jnp.concatenate([o_scr[g, cc] for cc in range(nc)], axis=1)
    y = jnp.dot(merged.astype(BF16), wout_ref[...], preferred_element_type=F32)
    mod = mod_ref[0]
    x1 = x_ref[0] + mod[2:3] * y
    x1_ref[0] = x1
    _ffn_input_and_logits(x1, mod, g2_ref, wr_ref, br_ref, h2_ref, lg_ref)


def _attn_out(outs, lses, x, mod, w_out, g_ffn, wr, br):
    b, s, d = x.shape
    ts = min(ROW_TILE, s)
    nst = s // ts
    nj = d // (2 * V7X_LANES)
    const = lambda bi, si: (0, 0)
    tok = lambda bi, si: (bi * nst + si, 0)
    stream = lambda bi, si: (bi, 0, si, 0)
    return pl.pallas_call(
        _attn_out_kernel,
        out_shape=(jax.ShapeDtypeStruct((b, s, d), F32),
                   jax.ShapeDtypeStruct((b * s * nj, V7X_LANES), jnp.int32),
                   jax.ShapeDtypeStruct((b * s, V7X_LANES), F32)),
        grid=(b, nst),
        in_specs=[pl.BlockSpec((1, dil, ts // dil, ATTN_OUT_DIM), stream) for _, dil in DILATED_GROUPS]
                 + [pl.BlockSpec((1, dil, ts // dil, V7X_LANES), stream) for _, dil in DILATED_GROUPS]
                 + [pl.BlockSpec((1, ts, d), lambda bi, si: (bi, si, 0)),
                    pl.BlockSpec((1, 6, d), lambda bi, si: (bi, 0, 0)),
                    pl.BlockSpec((ATTN_OUT_DIM, d), const),
                    pl.BlockSpec((1, d), const),
                    pl.BlockSpec((d, V7X_LANES), const),
                    pl.BlockSpec((1, V7X_LANES), const)],
        out_specs=(pl.BlockSpec((1, ts, d), lambda bi, si: (bi, si, 0)),
                   pl.BlockSpec((ts * nj, V7X_LANES), tok),
                   pl.BlockSpec((ts, V7X_LANES), tok)),
        scratch_shapes=[pltpu.VMEM((N_DIL_GROUPS, ATTN_OUT_DIM // V7X_LANES, ts, V7X_LANES), F32),
                        pltpu.VMEM((N_DIL_GROUPS, ts, V7X_LANES), F32)],
        compiler_params=_params(2),
        name="attn_out",
    )(*outs, *lses, x, mod, w_out, g_ffn, wr, br)


def _router_params(w_grp, b_grp, w_exp, b_exp):
    d = w_grp.shape[0]
    w_e = jnp.transpose(w_exp, (1, 0, 2)).reshape(d, N_EXPERTS)
    pad1 = ROUTER_LANE_OFFSET - N_GROUPS
    pad2 = V7X_LANES - ROUTER_LANE_OFFSET - N_EXPERTS
    wr = jnp.concatenate([w_grp, jnp.zeros((d, pad1), F32), w_e, jnp.zeros((d, pad2), F32)], axis=1)
    br = jnp.concatenate([b_grp, jnp.zeros((pad1,), F32), b_exp.reshape(-1), jnp.zeros((pad2,), F32)])
    return wr.astype(BF16), br.reshape(1, V7X_LANES).astype(F32)


def kernel(x, c, norm_mix_g, norm_ffn_g, ada_w, ada_b, conv_in_w, conv_w, conv_out_w, attn_in_w,
           attn_out_w, router_grp_w, router_grp_b, router_exp_w, router_exp_b, exp_gate_w, exp_up_w,
           exp_down_w, final_norm_g):
    b, s, d = x.shape
    depth = ada_w.shape[0]
    assert depth == 2 and s % (DILATED_GROUPS[-1][1] * ATTN_SPAN) == 0 and d % (2 * V7X_LANES) == 0
    mod = _ada_modulation(c, ada_w, ada_b).reshape(depth, b, 6, d)
    g_fin = final_norm_g.reshape(1, d)

    wr, br = _router_params(router_grp_w[0], router_grp_b[0], router_exp_w[0], router_exp_b[0])
    x1, h2, logits = _conv_mixer(x, mod[0], norm_mix_g[0:1], conv_in_w[0].astype(BF16), conv_w[0],
                                 conv_out_w[0].astype(BF16), norm_ffn_g[0:1], wr, br)
    x2 = _hierarchical_moe(x1, h2, logits, mod[0], g_fin, exp_gate_w[0], exp_up_w[0], exp_down_w[0],
                           final=False)

    wr, br = _router_params(router_grp_w[1], router_grp_b[1], router_exp_w[1], router_exp_b[1])
    qkv = _qkv_proj(x2, mod[1], norm_mix_g[1:2], attn_in_w[0].astype(BF16))
    outs, lses = zip(*[_dilated_attention(qkv_g) for qkv_g in qkv])
    x3, h2, logits = _attn_out(outs, lses, x2, mod[1], attn_out_w[0].astype(BF16), norm_ffn_g[1:2],
                               wr, br)
    return _hierarchical_moe(x3, h2, logits, mod[1], g_fin, exp_gate_w[1], exp_up_w[1], exp_down_w[1],
                             final=True)
```

```python
import functools

import jax
import jax.numpy as jnp
from jax import lax
from jax.experimental import pallas as pl
from jax.experimental.pallas import tpu as pltpu

CONV_WIDTH = 3
DILATED_GROUPS = ((128, 1), (512, 4), (2048, 16))
N_DIL_GROUPS = len(DILATED_GROUPS)
ATTN_HEADS = 8
HEAD_DIM = 64
ATTN_OUT_DIM = ATTN_HEADS * HEAD_DIM
ATTN_IN_DIM = N_DIL_GROUPS * 3 * ATTN_OUT_DIM
ROT_DIM = HEAD_DIM // 4
ROPE_THETA = 500000.0
N_GROUPS = 4
EXPERTS_PER_GROUP = 8
N_EXPERTS = N_GROUPS * EXPERTS_PER_GROUP
NORM_EPS = 1e-6
ATTN_SPAN = 128

V7X_LANES = 128
V7X_VMEM_LIMIT_BYTES = 56 * 1024 * 1024

ROW_TILE = 512
CONV_COL_BLOCK = 256
ROUTE_TILE = 1024
MOE_BLOCK = 512
ATTN_Q_TILE = 512
ROUTER_LANE_OFFSET = 8
DMA_ROW_TILE = 1024
DMA_ISSUE_UNROLL = 8

F32 = jnp.float32
BF16 = jnp.bfloat16


def _params(n_axes):
    return pltpu.CompilerParams(dimension_semantics=("arbitrary",) * n_axes,
                                vmem_limit_bytes=V7X_VMEM_LIMIT_BYTES)


def _norm_mod(x, g, shift, scale):
    ms = jnp.mean(x * x, axis=-1, keepdims=True)
    y = x * lax.rsqrt(ms + NORM_EPS) * g
    return y * (1.0 + scale) + shift


def _pack_bf16_pair(a, b):
    return pltpu.pack_elementwise([a, b], packed_dtype=BF16)


def _unpack_bf16_pair(w):
    a = pltpu.unpack_elementwise(w, index=0, packed_dtype=BF16, unpacked_dtype=F32)
    b = pltpu.unpack_elementwise(w, index=1, packed_dtype=BF16, unpacked_dtype=F32)
    return a, b


def _store_row_groups(ref, x):
    rows, d = x.shape
    nj = d // (2 * V7X_LANES)
    for j in range(nj):
        lo = x[:, j * V7X_LANES:(j + 1) * V7X_LANES]
        hi = x[:, d // 2 + j * V7X_LANES:d // 2 + (j + 1) * V7X_LANES]
        ref[pl.ds(j, rows, stride=nj), :] = _pack_bf16_pair(lo, hi)


def _load_row_groups(ref, rows, nj):
    los, his = [], []
    for j in range(nj):
        lo, hi = _unpack_bf16_pair(ref[pl.ds(j, rows, stride=nj), :])
        los.append(lo)
        his.append(hi)
    return los + his


def _ffn_input_and_logits(x1, mod, g2_ref, wr_ref, br_ref, h2_ref, lg_ref):
    h2 = _norm_mod(x1, g2_ref[...], mod[3:4], mod[4:5])
    _store_row_groups(h2_ref, h2)
    lg_ref[...] = jnp.dot(h2.astype(BF16), wr_ref[...], preferred_element_type=F32) + br_ref[...]


def _ada_kernel(c_ref, w_ref, b_ref, o_ref):
    c = c_ref[...]
    ca = c * jax.nn.sigmoid(c)
    o_ref[0] = jnp.dot(ca, w_ref[0], preferred_element_type=F32,
                       precision=lax.Precision.HIGHEST) + b_ref[0]


def _ada_modulation(c, ada_w, ada_b):
    depth, d, n = ada_w.shape
    b = c.shape[0]
    tn = d
    return pl.pallas_call(
        _ada_kernel,
        out_shape=jax.ShapeDtypeStruct((depth, b, n), F32),
        grid=(depth, n // tn),
        in_specs=[pl.BlockSpec((b, d), lambda i, j: (0, 0)),
                  pl.BlockSpec((1, d, tn), lambda i, j: (i, 0, j)),
                  pl.BlockSpec((1, 1, tn), lambda i, j: (i, 0, j))],
        out_specs=pl.BlockSpec((1, b, tn), lambda i, j: (i, 0, j)),
        compiler_params=_params(2),
        name="ada_modulation",
    )(c, ada_w, ada_b.reshape(depth, 1, n))


def _conv_mixer_kernel(x_ref, mod_ref, g_ref, win_ref, cw_ref, wout_ref, g2_ref, wr_ref, br_ref,
                       x1_ref, h2_ref, lg_ref, carry_ref, v_ref, *, cb):
    ts, d = x_ref.shape[1], x_ref.shape[2]

    @pl.when(pl.program_id(1) == 0)
    def _():
        carry_ref[...] = jnp.zeros_like(carry_ref)

    x = x_ref[0]
    mod = mod_ref[0]
    h = _norm_mod(x, g_ref[...], mod[0:1], mod[1:2]).astype(BF16)
    row = lax.broadcasted_iota(jnp.int32, (ts, cb), 0)
    for j in range(d // cb):
        lo = j * cb
        bj = jnp.dot(h, win_ref[:, lo:lo + cb], preferred_element_type=F32)
        cj = jnp.dot(h, win_ref[:, d + lo:d + lo + cb], preferred_element_type=F32)
        uj = jnp.dot(h, win_ref[:, 2 * d + lo:2 * d + lo + cb], preferred_element_type=F32)
        z = cj * uj
        prev = carry_ref[:, lo:lo + cb]
        z1 = jnp.where(row == 0, prev[7:8], pltpu.roll(z, 1, 0))
        z2 = jnp.where(row == 0, prev[6:7], jnp.where(row == 1, prev[7:8], pltpu.roll(z, 2, 0)))
        zc = cw_ref[0:1, lo:lo + cb] * z2 + cw_ref[1:2, lo:lo + cb] * z1 + cw_ref[2:3, lo:lo + cb] * z
        carry_ref[:, lo:lo + cb] = z[ts - 8:ts]
        v_ref[:, lo:lo + cb] = (bj * zc).astype(BF16)
    y = jnp.dot(v_ref[...], wout_ref[...], preferred_element_type=F32)
    x1 = x + mod[2:3] * y
    x1_ref[0] = x1
    _ffn_input_and_logits(x1, mod, g2_ref, wr_ref, br_ref, h2_ref, lg_ref)


def _conv_mixer(x, mod, g_mix, w_in, conv_w, w_out, g_ffn, wr, br):
    b, s, d = x.shape
    ts = min(ROW_TILE, s)
    cb = min(CONV_COL_BLOCK, d)
    nst = s // ts
    nj = d // (2 * V7X_LANES)
    const = lambda bi, si: (0, 0)
    return pl.pallas_call(
        functools.partial(_conv_mixer_kernel, cb=cb),
        out_shape=(jax.ShapeDtypeStruct((b, s, d), F32),
                   jax.ShapeDtypeStruct((b * s * nj, V7X_LANES), jnp.uint32),
                   jax.ShapeDtypeStruct((b * s, V7X_LANES), F32)),
        grid=(b, nst),
        in_specs=[pl.BlockSpec((1, ts, d), lambda bi, si: (bi, si, 0)),
                  pl.BlockSpec((1, 6, d), lambda bi, si: (bi, 0, 0)),
                  pl.BlockSpec((1, d), const),
                  pl.BlockSpec((d, 3 * d), const),
                  pl.BlockSpec((CONV_WIDTH, d), const),
                  pl.BlockSpec((d, d), const),
                  pl.BlockSpec((1, d), const),
                  pl.BlockSpec((d, V7X_LANES), const),
                  pl.BlockSpec((1, V7X_LANES), const)],
        out_specs=(pl.BlockSpec((1, ts, d), lambda bi, si: (bi, si, 0)),
                   pl.BlockSpec((ts * nj, V7X_LANES), lambda bi, si: (bi * nst + si, 0)),
                   pl.BlockSpec((ts, V7X_LANES), lambda bi, si: (bi * nst + si, 0))),
        scratch_shapes=[pltpu.VMEM((8, d), F32), pltpu.VMEM((ts, d), BF16)],
        compiler_params=_params(2),
        name="conv_mixer",
    )(x, mod, g_mix, w_in, conv_w, w_out, g_ffn, wr, br)


def _route_kernel(lg_ref, tri_ref, plan_ref, cnt_ref, run_ref, pst_ref, *, blk):
    ph, i = pl.program_id(0), pl.program_id(1)
    tr = lg_ref.shape[0]
    lanes = lg_ref.shape[1]

    @pl.when((ph == 0) & (i == 0))
    def _():
        run_ref[...] = jnp.zeros_like(run_ref)

    lg = lg_ref[...]
    lane = lax.broadcasted_iota(jnp.int32, (tr, lanes), 1)
    big = jnp.int32(lanes)
    neg = jnp.float32(-jnp.inf)
    gmask = lane < N_GROUPS
    gl = jnp.where(gmask, lg, neg)
    mg = jnp.max(gl, axis=1, keepdims=True)
    gidx = jnp.min(jnp.where(gl == mg, lane, big), axis=1, keepdims=True)
    den_g = jnp.sum(jnp.where(gmask, jnp.exp(lg - mg), 0.0), axis=1, keepdims=True)
    grp_p = 1.0 / den_g
    lo = ROUTER_LANE_OFFSET + EXPERTS_PER_GROUP * gidx
    el = jnp.where((lane >= lo) & (lane < lo + EXPERTS_PER_GROUP), lg, neg)
    l1 = jnp.max(el, axis=1, keepdims=True)
    i1 = jnp.min(jnp.where(el == l1, lane, big), axis=1, keepdims=True)
    el2 = jnp.where(lane == i1, neg, el)
    l2 = jnp.max(el2, axis=1, keepdims=True)
    i2 = jnp.min(jnp.where(el2 == l2, lane, big), axis=1, keepdims=True)
    r = jnp.exp(l2 - l1)
    w1 = grp_p / (1.0 + r)
    w2 = grp_p * r / (1.0 + r)
    oh1 = (lane == i1 - ROUTER_LANE_OFFSET).astype(F32)
    oh2 = (lane == i2 - ROUTER_LANE_OFFSET).astype(F32)
    tot1 = jnp.sum(oh1, axis=0, keepdims=True)
    tot2 = jnp.sum(oh2, axis=0, keepdims=True)
    run = run_ref[0:1, :]

    @pl.when(ph == 0)
    def _():
        new_run = run + tot1 + tot2
        run_ref[...] = jnp.broadcast_to(new_run, run_ref.shape)

        @pl.when(i == pl.num_programs(1) - 1)
        def _():
            cnt = jnp.broadcast_to(new_run, run_ref.shape)
            padded = jnp.floor((cnt + (blk - 1)) / blk) * blk
            l8 = lax.broadcasted_iota(jnp.int32, cnt.shape, 1)
            csum = padded
            k = 1
            while k < N_EXPERTS:
                csum = csum + jnp.where(l8 >= k, pltpu.roll(csum, k, 1), 0.0)
                k *= 2
            pst_ref[...] = csum - padded
            run_ref[...] = jnp.zeros_like(run_ref)
            r8 = lax.broadcasted_iota(jnp.int32, cnt.shape, 0)
            cnt_ref[...] = jnp.where(r8 == 0, cnt, jnp.where(r8 == 1, csum, jnp.where(r8 == 2, padded, 0.0)))

    @pl.when(ph == 1)
    def _():
        oh = jnp.concatenate([oh1, oh2], axis=1).astype(BF16)
        before = jnp.dot(tri_ref[...], oh, preferred_element_type=F32)
        base = pst_ref[0:1, :] + run
        pos1 = jnp.sum(oh1 * (base + before[:, :lanes]), axis=1, keepdims=True)
        pos2 = jnp.sum(oh2 * (base + tot1 + before[:, lanes:]), axis=1, keepdims=True)
        run_ref[...] = jnp.broadcast_to(run + tot1 + tot2, run_ref.shape)
        plan_ref[...] = jnp.where(lane == 0, pos1, jnp.where(lane == 1, pos2,
                                  jnp.where(lane == 2, w1, jnp.where(lane == 3, w2, 0.0))))


def _route(logits):
    t, lanes = logits.shape
    tr = min(ROUTE_TILE, t)
    nt = t // tr
    tri = (jnp.arange(tr)[None, :] < jnp.arange(tr)[:, None]).astype(BF16)
    return pl.pallas_call(
        functools.partial(_route_kernel, blk=MOE_BLOCK),
        out_shape=(jax.ShapeDtypeStruct((t, lanes), F32),
                   jax.ShapeDtypeStruct((8, lanes), F32)),
        grid=(2, nt),
        in_specs=[pl.BlockSpec((tr, lanes), lambda ph, i: (i, 0)),
                  pl.BlockSpec((tr, tr), lambda ph, i: (0, 0))],
        out_specs=(pl.BlockSpec((tr, lanes), lambda ph, i: (i * ph, 0)),
                   pl.BlockSpec((8, lanes), lambda ph, i: (0, 0))),
        scratch_shapes=[pltpu.VMEM((8, lanes), F32), pltpu.VMEM((8, lanes), F32)],
        compiler_params=_params(2),
        name="route",
    )(logits, tri)


def _dispatch_kernel(last_ref, row_ref, h2_ref, xs_hbm, zero_buf, sem, zero_sem, *, nj):
    ts = row_ref.shape[1]
    blk_rows = zero_buf.shape[0]

    @pl.when(pl.program_id(0) == 0)
    def _():
        z = jnp.zeros(zero_buf.shape, F32)
        zero_buf[...] = _pack_bf16_pair(z, z)

        def blank(e):
            dst = xs_hbm.at[pl.ds(pl.multiple_of(last_ref[e], nj), blk_rows)]
            return pltpu.make_async_copy(zero_buf, dst, zero_sem)

        def zstart(e, carry):
            @pl.when(last_ref[e] >= 0)
            def _():
                blank(e).start()
            return carry

        def zwait(e, carry):
            @pl.when(last_ref[e] >= 0)
            def _():
                blank(e).wait()
            return carry

        def tail(bi):
            dst = xs_hbm.at[pl.ds(pl.multiple_of(bi * blk_rows, blk_rows), blk_rows)]
            return pltpu.make_async_copy(zero_buf, dst, zero_sem)

        def tstart(bi, carry):
            tail(bi).start()
            return carry

        def twait(bi, carry):
            tail(bi).wait()
            return carry

        n_blocks = xs_hbm.shape[0] // blk_rows
        lax.fori_loop(0, N_EXPERTS, zstart, 0)
        lax.fori_loop(last_ref[N_EXPERTS], n_blocks, tstart, 0)
        lax.fori_loop(0, N_EXPERTS, zwait, 0)
        lax.fori_loop(last_ref[N_EXPERTS], n_blocks, twait, 0)

    def start(t, carry):
        src = h2_ref.at[pl.ds(pl.multiple_of(t * nj, nj), nj)]
        for k in range(2):
            pltpu.make_async_copy(src, xs_hbm.at[pl.ds(pl.multiple_of(row_ref[k, t], nj), nj)],
                                  sem).start(priority=k)
        return carry

    lax.fori_loop(0, ts, start, 0, unroll=DMA_ISSUE_UNROLL)
    for _ in range(2):
        pltpu.make_async_copy(h2_ref, xs_hbm.at[pl.ds(0, ts * nj)], sem).wait()


def _dispatch(last_row, row_t, h2, n_rows, nj):
    t = row_t.shape[1]
    ts = min(DMA_ROW_TILE, t)
    grid_spec = pltpu.PrefetchScalarGridSpec(
        num_scalar_prefetch=1,
        grid=(t // ts,),
        in_specs=[pl.BlockSpec((2, ts), lambda i, last: (0, i), memory_space=pltpu.SMEM),
                  pl.BlockSpec((ts * nj, V7X_LANES), lambda i, last: (i, 0))],
        out_specs=pl.BlockSpec(memory_space=pl.ANY),
        scratch_shapes=[pltpu.VMEM((MOE_BLOCK * nj, V7X_LANES), h2.dtype),
                        pltpu.SemaphoreType.DMA(()), pltpu.SemaphoreType.DMA(())])
    return pl.pallas_call(
        functools.partial(_dispatch_kernel, nj=nj),
        out_shape=jax.ShapeDtypeStruct((n_rows * nj, V7X_LANES), h2.dtype),
        grid_spec=grid_spec,
        compiler_params=_params(1),
        name="dispatch",
    )(last_row, row_t, h2)


def _experts_kernel(be_ref, nact_ref, xs_ref, wg_ref, wu_ref, wd_ref, ys_ref, wgb, wub, wdb, *, nj):
    i = pl.program_id(0)
    blk = xs_ref.shape[0] // nj

    @pl.when(i < nact_ref[0])
    def _():
        prev = be_ref[jnp.maximum(i - 1, 0)]

        @pl.when((i == 0) | (be_ref[i] != prev))
        def _():
            wgb[...] = wg_ref[0, 0].astype(BF16)
            wub[...] = wu_ref[0, 0].astype(BF16)
            wdb[...] = wd_ref[0, 0].astype(BF16)

        x = jnp.concatenate([p.astype(BF16) for p in _load_row_groups(xs_ref, blk, nj)], axis=1)
        g = jnp.dot(x, wgb[...], preferred_element_type=F32)
        u = jnp.dot(x, wub[...], preferred_element_type=F32)
        a = (g * jax.nn.sigmoid(g) * u).astype(BF16)
        y = jnp.dot(a, wdb[...], preferred_element_type=F32)
        _store_row_groups(ys_ref, y)

    @pl.when(i >= nact_ref[0])
    def _():
        z = jnp.zeros(ys_ref.shape, F32)
        ys_ref[...] = _pack_bf16_pair(z, z)


def _experts(block_e, n_active, xs, w_gate, w_up, w_down, layer):
    _, _, d, de = w_gate.shape
    nj = d // (2 * V7X_LANES)
    n_blocks = xs.shape[0] // (MOE_BLOCK * nj)
    grid_spec = pltpu.PrefetchScalarGridSpec(
        num_scalar_prefetch=2,
        grid=(n_blocks,),
        in_specs=[pl.BlockSpec((MOE_BLOCK * nj, V7X_LANES), lambda i, be, na: (i, 0)),
                  pl.BlockSpec((1, 1, d, de), lambda i, be, na: (layer, be[i], 0, 0)),
                  pl.BlockSpec((1, 1, d, de), lambda i, be, na: (layer, be[i], 0, 0)),
                  pl.BlockSpec((1, 1, de, d), lambda i, be, na: (layer, be[i], 0, 0))],
        out_specs=pl.BlockSpec((MOE_BLOCK * nj, V7X_LANES), lambda i, be, na: (i, 0)),
        scratch_shapes=[pltpu.VMEM((d, de), BF16), pltpu.VMEM((d, de), BF16),
                        pltpu.VMEM((de, d), BF16)])
    return pl.pallas_call(
        functools.partial(_experts_kernel, nj=nj),
        out_shape=jax.ShapeDtypeStruct(xs.shape, jnp.uint32),
        grid_spec=grid_spec,
        compiler_params=_params(1),
        name="experts",
    )(block_e, n_active, xs, w_gate, w_up, w_down)


def _combine_kernel(row_ref, x_ref, plan_ref, mod_ref, gfin_ref, ys_hbm, out_ref, buf, sem, *, final, nj):
    ts, d = x_ref.shape[1], x_ref.shape[2]

    def start(t, carry):
        dst = pl.ds(pl.multiple_of(t * nj, nj), nj)
        for k in range(2):
            pltpu.make_async_copy(ys_hbm.at[pl.ds(pl.multiple_of(row_ref[k, t], nj), nj)],
                                  buf.at[k, dst], sem).start(priority=k)
        return carry

    lax.fori_loop(0, ts, start, 0, unroll=DMA_ISSUE_UNROLL)
    for k in range(2):
        pltpu.make_async_copy(ys_hbm.at[pl.ds(0, ts * nj)], buf.at[k], sem).wait()
    w1 = plan_ref[:, 2:3]
    w2 = plan_ref[:, 3:4]
    p1 = _load_row_groups(buf.at[0], ts, nj)
    p2 = _load_row_groups(buf.at[1], ts, nj)
    moe = jnp.concatenate([w1 * a + w2 * b for a, b in zip(p1, p2)], axis=1)
    x2 = x_ref[0] + mod_ref[0][5:6] * moe
    if final:
        ms = jnp.mean(x2 * x2, axis=-1, keepdims=True)
        x2 = x2 * lax.rsqrt(ms + NORM_EPS) * gfin_ref[...]
    out_ref[0] = x2


def _combine(row_t, x1, plan, mod, g_final, ys, *, final):
    b, s, d = x1.shape
    ts = min(DMA_ROW_TILE, s)
    nst = s // ts
    nj = d // (2 * V7X_LANES)
    return pl.pallas_call(
        functools.partial(_combine_kernel, final=final, nj=nj),
        out_shape=jax.ShapeDtypeStruct((b, s, d), F32),
        grid=(b, nst),
        in_specs=[pl.BlockSpec((2, ts), lambda bi, si: (0, bi * nst + si), memory_space=pltpu.SMEM),
                  pl.BlockSpec((1, ts, d), lambda bi, si: (bi, si, 0)),
                  pl.BlockSpec((ts, plan.shape[1]), lambda bi, si: (bi * nst + si, 0)),
                  pl.BlockSpec((1, 6, d), lambda bi, si: (bi, 0, 0)),
                  pl.BlockSpec((1, d), lambda bi, si: (0, 0)),
                  pl.BlockSpec(memory_space=pl.ANY)],
        out_specs=pl.BlockSpec((1, ts, d), lambda bi, si: (bi, si, 0)),
        scratch_shapes=[pltpu.VMEM((2, ts * nj, V7X_LANES), jnp.uint32), pltpu.SemaphoreType.DMA(())],
        compiler_params=_params(2),
        name="combine_final" if final else "combine",
    )(row_t, x1, plan, mod, g_final, ys)


def _hierarchical_moe(x1, h2, logits, mod, g_final, w_gate, w_up, w_down, *, layer, final):
    t = logits.shape[0]
    nj = h2.shape[0] // t
    plan, cnt = _route(logits)
    n_blocks = -(-(t * 2) // MOE_BLOCK) + N_EXPERTS
    seg_end, seg_rows = cnt[1, :N_EXPERTS], cnt[2, :N_EXPERTS]
    pends = (seg_end * (1.0 / MOE_BLOCK)).astype(jnp.int32)
    block_e = jnp.minimum(jnp.sum(pends[None, :] <= jnp.arange(n_blocks)[:, None], axis=1),
                          N_EXPERTS - 1).astype(jnp.int32)
    n_active = pends[-1:]
    last_row = jnp.where(seg_rows > 0, (seg_end - MOE_BLOCK) * nj, -1.0).astype(jnp.int32)
    last_row = jnp.concatenate([last_row, n_active])
    row_t = (plan[:, 0:2] * nj).astype(jnp.int32).T
    xs = _dispatch(last_row, row_t, h2, n_blocks * MOE_BLOCK, nj)
    ys = _experts(block_e, n_active, xs, w_gate, w_up, w_down, layer)
    return _combine(row_t, x1, plan, mod, g_final, ys, final=final)


def _qkv_kernel(x_ref, mod_ref, g_ref, w_ref, *refs):
    tabs, (o0_ref, o1_ref, o2_ref), scr = refs[:9], refs[9:12], refs[12]
    ts, d = x_ref.shape[1], x_ref.shape[2]
    mod = mod_ref[0]
    hn = _norm_mod(x_ref[0], g_ref[...], mod[0:1], mod[1:2])
    wd = ATTN_OUT_DIM
    rep = wd // V7X_LANES
    nc = d // V7X_LANES
    for g, o_ref in enumerate((o0_ref, o1_ref, o2_ref)):
        dil = DILATED_GROUPS[g][1]
        n = ts // dil
        if dil == 1:
            h = hn.astype(BF16)
        else:
            if g == 1:
                for cc in range(nc):
                    scr[cc] = hn[:, cc * V7X_LANES:(cc + 1) * V7X_LANES]
            h = jnp.concatenate(
                [jnp.concatenate([scr[cc, pl.ds(r, n, stride=dil), :] for cc in range(nc)], axis=1)
                 for r in range(dil)], axis=0).astype(BF16)
        cos, sa, sb = (jnp.tile(t[...], (1, rep)) for t in tabs[3 * g:3 * g + 3])
        for j in range(3):
            slab = g * 3 + j
            acc = jnp.dot(h, w_ref[:, slab * wd:(slab + 1) * wd], preferred_element_type=F32)
            if j != 2:
                acc = (acc * cos + pltpu.roll(acc, ROT_DIM // 2, 1) * sa
                       + pltpu.roll(acc, wd - ROT_DIM // 2, 1) * sb)
            acc = acc.astype(BF16)
            for r in range(dil):
                o_ref[0, r, :, j * wd:(j + 1) * wd] = acc[r * n:(r + 1) * n]


def _rope_tables(s, ts):
    half = ROT_DIM // 2
    pos = jnp.arange(s, dtype=F32)
    inv_freq = jnp.power(jnp.float32(ROPE_THETA), -jnp.arange(0, ROT_DIM, 2, dtype=F32) / ROT_DIM)
    ang = pos[:, None] * inv_freq[None, :]
    cos, sin = jnp.cos(ang), jnp.sin(ang)
    m = jnp.arange(V7X_LANES) % HEAD_DIM
    cos_l = jnp.where(m[None, :] < ROT_DIM, cos[:, m % half], 1.0)
    sa_l = jnp.where((m[None, :] >= half) & (m[None, :] < ROT_DIM), sin[:, m % half], 0.0)
    sb_l = jnp.where(m[None, :] < half, -sin[:, m % half], 0.0)
    tabs = []
    for _, dil in DILATED_GROUPS:
        for t in (cos_l, sa_l, sb_l):
            t = t.astype(F32).reshape(s // ts, ts // dil, dil, V7X_LANES)
            tabs.append(jnp.transpose(t, (0, 2, 1, 3)).reshape(s, V7X_LANES))
    return tabs


def _qkv_proj(x, mod, g_mix, w_in):
    b, s, d = x.shape
    ts = min(ROW_TILE, s)
    const = lambda si, bi: (0, 0)
    tab = pl.BlockSpec((ts, V7X_LANES), lambda si, bi: (si, 0))
    wd3 = 3 * ATTN_OUT_DIM
    return pl.pallas_call(
        _qkv_kernel,
        out_shape=tuple(jax.ShapeDtypeStruct((b, dil, s // dil, wd3), BF16) for _, dil in DILATED_GROUPS),
        grid=(s // ts, b),
        in_specs=[pl.BlockSpec((1, ts, d), lambda si, bi: (bi, si, 0)),
                  pl.BlockSpec((1, 6, d), lambda si, bi: (bi, 0, 0)),
                  pl.BlockSpec((1, d), const),
                  pl.BlockSpec((d, ATTN_IN_DIM), const)] + [tab] * 9,
        out_specs=tuple(pl.BlockSpec((1, dil, ts // dil, wd3), lambda si, bi: (bi, 0, si, 0))
                        for _, dil in DILATED_GROUPS),
        scratch_shapes=[pltpu.VMEM((d // V7X_LANES, ts, V7X_LANES), F32)],
        compiler_params=_params(2),
        name="qkv_proj",
    )(x, mod, g_mix, w_in, *_rope_tables(s, ts))


def _attn_kernel(q_ref, kc_ref, kp_ref, vc_ref, vp_ref, o_ref, lse_ref):
    tq = q_ref.shape[2]
    sp = ATTN_SPAN
    nl = lse_ref.shape[3]
    first_tile = pl.program_id(2) == 0
    kext = jnp.concatenate([kp_ref[0, 0], kc_ref[0, 0]], axis=0)
    vext = jnp.concatenate([vp_ref[0, 0], vc_ref[0, 0]], axis=0)
    q_t = (q_ref[0, 0].astype(F32) * (HEAD_DIM ** -0.5)).T.astype(BF16)
    v_t = vext.astype(F32).T.astype(BF16)
    kj = lax.broadcasted_iota(jnp.int32, (2 * sp, sp), 0)
    qi = lax.broadcasted_iota(jnp.int32, (2 * sp, sp), 1)
    neg = jnp.float32(-jnp.inf)
    band = jnp.where(kj >= qi, jnp.where(kj <= qi + sp, 0.0, neg), neg)
    zeros_half = jnp.zeros((HEAD_DIM, sp), BF16)
    head_row = lax.broadcasted_iota(jnp.int32, (ATTN_HEADS, sp), 0)
    out_cols, lse_cols = [], []
    for i in range(tq // sp):
        if i == 0:
            bias = jnp.where(first_tile & (kj < sp), neg, band)
        else:
            bias = band
        scores = []
        for h in range(ATTN_HEADS):
            pair = h // 2
            k_pair = kext[i * sp:(i + 2) * sp, pair * 2 * HEAD_DIM:(pair + 1) * 2 * HEAD_DIM]
            q_h = q_t[h * HEAD_DIM:(h + 1) * HEAD_DIM, i * sp:(i + 1) * sp]
            q_m = jnp.concatenate([q_h, zeros_half] if h % 2 == 0 else [zeros_half, q_h], axis=0)
            scores.append(jnp.dot(k_pair, q_m, preferred_element_type=F32) + bias)
        probs, dens = [], []
        lse_t = jnp.zeros((ATTN_HEADS, sp), F32)
        for h, sc in enumerate(scores):
            m = jnp.max(sc, axis=0, keepdims=True)
            p = jnp.exp(sc - m)
            den = jnp.sum(p, axis=0, keepdims=True)
            probs.append(p.astype(BF16))
            dens.append(den)
            lse_t = jnp.where(head_row == h, m + jnp.log(den), lse_t)
        o_rows = []
        for h in range(ATTN_HEADS):
            v_h = v_t[h * HEAD_DIM:(h + 1) * HEAD_DIM, i * sp:(i + 2) * sp]
            o_rows.append(jnp.dot(v_h, probs[h], preferred_element_type=F32) / dens[h])
        out_cols.append(jnp.concatenate(o_rows, axis=0))
        lse_cols.append(jnp.concatenate([lse_t, jnp.zeros((nl - ATTN_HEADS, sp), F32)], axis=0))
    o_ref[0, 0] = jnp.concatenate(out_cols, axis=1).T.astype(BF16)
    lse_ref[0, 0] = jnp.concatenate(lse_cols, axis=1).T


def _dilated_attention(qkv_g):
    b, dil, L, _ = qkv_g.shape
    tq = min(ATTN_Q_TILE, L)
    sub = tq // ATTN_SPAN
    wd = ATTN_OUT_DIM

    def cur(j):
        return pl.BlockSpec((1, 1, tq, wd), lambda bi, r, lb: (bi, r, lb, j))

    def prev(j):
        return pl.BlockSpec((1, 1, ATTN_SPAN, wd),
                            lambda bi, r, lb: (bi, r, jnp.maximum(lb * sub - 1, 0), j))

    return pl.pallas_call(
        _attn_kernel,
        out_shape=(jax.ShapeDtypeStruct((b, dil, L, wd), BF16),
                   jax.ShapeDtypeStruct((b, dil, L, V7X_LANES), F32)),
        grid=(b, dil, L // tq),
        in_specs=[cur(0), cur(1), prev(1), cur(2), prev(2)],
        out_specs=(pl.BlockSpec((1, 1, tq, wd), lambda bi, r, lb: (bi, r, lb, 0)),
                   pl.BlockSpec((1, 1, tq, V7X_LANES), lambda bi, r, lb: (bi, r, lb, 0))),
        compiler_params=_params(3),
        name=f"dilated_attention_{dil}",
    )(qkv_g, qkv_g, qkv_g, qkv_g, qkv_g)


def _attn_out_kernel(o0_ref, o1_ref, o2_ref, l0_ref, l1_ref, l2_ref, x_ref, mod_ref, wout_ref, e_ref,
                     g2_ref, wr_ref, br_ref, x1_ref, h2_ref, lg_ref, o_scr, l_scr):
    ts = x_ref.shape[1]
    nc = ATTN_OUT_DIM // V7X_LANES
    for g, (o_ref, l_ref) in enumerate(((o0_ref, l0_ref), (o1_ref, l1_ref), (o2_ref, l2_ref))):
        dil = DILATED_GROUPS[g][1]
        for r in range(dil):
            rows = pl.ds(r, ts // dil, stride=dil) if dil > 1 else slice(None)
            o_r = o_ref[0, r].astype(F32)
            for cc in range(nc):
                o_scr[g, cc, rows, :] = o_r[:, cc * V7X_LANES:(cc + 1) * V7X_LANES]
            l_scr[g, rows, :] = l_ref[0, r]
    l0, l1, l2 = l_scr[0], l_scr[1], l_scr[2]
    m = jnp.maximum(jnp.maximum(l0, l1), l2)
    es = (jnp.exp(l0 - m), jnp.exp(l1 - m), jnp.exp(l2 - m))
    den = es[0] + es[1] + es[2]
    merged = jnp.zeros((ts, ATTN_OUT_DIM), F32)
    for g in range(N_DIL_GROUPS):
        w = es[g] / den
        w_hi = w.astype(BF16)
        w_lo = (w - w_hi.astype(F32)).astype(BF16)
        wfull = (jnp.dot(w_hi, e_ref[...], preferred_element_type=F32)
                 + jnp.dot(w_lo, e_ref[...], preferred_element_type=F32))
        merged = merged + wfull * jnp.concatenate([o_scr[g, cc] for cc in range(nc)], axis=1)
    y = jnp.dot(merged.astype(BF16), wout_ref[...], preferred_element_type=F32)
    mod = mod_ref[0]
    x1 = x_ref[0] + mod[2:3] * y
    x1_ref[0] = x1
    _ffn_input_and_logits(x1, mod, g2_ref, wr_ref, br_ref, h2_ref, lg_ref)


def _attn_out(outs, lses, x, mod, w_out, g_ffn, wr, br):
    b, s, d = x.shape
    ts = min(ROW_TILE, s)
    nst = s // ts
    nj = d // (2 * V7X_LANES)
    const = lambda bi, si: (0, 0)
    tok = lambda bi, si: (bi * nst + si, 0)
    stream = lambda bi, si: (bi, 0, si, 0)
    expand = (jnp.arange(V7X_LANES)[:, None] == jnp.arange(ATTN_OUT_DIM)[None, :] // HEAD_DIM).astype(BF16)
    return pl.pallas_call(
        _attn_out_kernel,
        out_shape=(jax.ShapeDtypeStruct((b, s, d), F32),
                   jax.ShapeDtypeStruct((b * s * nj, V7X_LANES), jnp.uint32),
                   jax.ShapeDtypeStruct((b * s, V7X_LANES), F32)),
        grid=(b, nst),
        in_specs=[pl.BlockSpec((1, dil, ts // dil, ATTN_OUT_DIM), stream) for _, dil in DILATED_GROUPS]
                 + [pl.BlockSpec((1, dil, ts // dil, V7X_LANES), stream) for _, dil in DILATED_GROUPS]
                 + [pl.BlockSpec((1, ts, d), lambda bi, si: (bi, si, 0)),
                    pl.BlockSpec((1, 6, d), lambda bi, si: (bi, 0, 0)),
                    pl.BlockSpec((ATTN_OUT_DIM, d), const),
                    pl.BlockSpec((V7X_LANES, ATTN_OUT_DIM), const),
                    pl.BlockSpec((1, d), const),
                    pl.BlockSpec((d, V7X_LANES), const),
                    pl.BlockSpec((1, V7X_LANES), const)],
        out_specs=(pl.BlockSpec((1, ts, d), lambda bi, si: (bi, si, 0)),
                   pl.BlockSpec((ts * nj, V7X_LANES), tok),
                   pl.BlockSpec((ts, V7X_LANES), tok)),
        scratch_shapes=[pltpu.VMEM((N_DIL_GROUPS, ATTN_OUT_DIM // V7X_LANES, ts, V7X_LANES), F32),
                        pltpu.VMEM((N_DIL_GROUPS, ts, V7X_LANES), F32)],
        compiler_params=_params(2),
        name="attn_out",
    )(*outs, *lses, x, mod, w_out, expand, g_ffn, wr, br)


def _router_params(w_grp, b_grp, w_exp, b_exp):
    d = w_grp.shape[0]
    w_e = jnp.transpose(w_exp, (1, 0, 2)).reshape(d, N_EXPERTS)
    pad1 = ROUTER_LANE_OFFSET - N_GROUPS
    pad2 = V7X_LANES - ROUTER_LANE_OFFSET - N_EXPERTS
    wr = jnp.concatenate([w_grp, jnp.zeros((d, pad1), F32), w_e, jnp.zeros((d, pad2), F32)], axis=1)
    br = jnp.concatenate([b_grp, jnp.zeros((pad1,), F32), b_exp.reshape(-1), jnp.zeros((pad2,), F32)])
    return wr.astype(BF16), br.reshape(1, V7X_LANES).astype(F32)


def kernel(x, c, norm_mix_g, norm_ffn_g, ada_w, ada_b, conv_in_w, conv_w, conv_out_w, attn_in_w,
           attn_out_w, router_grp_w, router_grp_b, router_exp_w, router_exp_b, exp_gate_w, exp_up_w,
           exp_down_w, final_norm_g):
    b, s, d = x.shape
    depth = ada_w.shape[0]
    assert depth == 2 and s % (DILATED_GROUPS[-1][1] * ATTN_SPAN) == 0 and d % (2 * V7X_LANES) == 0
    mod = _ada_modulation(c, ada_w, ada_b).reshape(depth, b, 6, d)
    g_fin = final_norm_g.reshape(1, d)

    wr, br = _router_params(router_grp_w[0], router_grp_b[0], router_exp_w[0], router_exp_b[0])
    x1, h2, logits = _conv_mixer(x, mod[0], norm_mix_g[0:1], conv_in_w[0].astype(BF16), conv_w[0],
                                 conv_out_w[0].astype(BF16), norm_ffn_g[0:1], wr, br)
    x2 = _hierarchical_moe(x1, h2, logits, mod[0], g_fin, exp_gate_w, exp_up_w, exp_down_w,
                           layer=0, final=False)

    wr, br = _router_params(router_grp_w[1], router_grp_b[1], router_exp_w[1], router_exp_b[1])
    qkv = _qkv_proj(x2, mod[1], norm_mix_g[1:2], attn_in_w[0].astype(BF16))
    outs, lses = zip(*[_dilated_attention(qkv_g) for qkv_g in qkv])
    x3, h2, logits = _attn_out(outs, lses, x2, mod[1], attn_out_w[0].astype(BF16), norm_ffn_g[1:2],
                               wr, br)
    return _hierarchical_moe(x3, h2, logits, mod[1], g_fin, exp_gate_w, exp_up_w, exp_down_w,
                             layer=1, final=True)
```

```python
import functools

import jax
import jax.numpy as jnp
from jax import lax
from jax.experimental import pallas as pl
from jax.experimental.pallas import tpu as pltpu

CONV_WIDTH = 3
DILATED_GROUPS = ((128, 1), (512, 4), (2048, 16))
N_DIL_GROUPS = len(DILATED_GROUPS)
ATTN_HEADS = 8
HEAD_DIM = 64
ATTN_OUT_DIM = ATTN_HEADS * HEAD_DIM
ATTN_IN_DIM = N_DIL_GROUPS * 3 * ATTN_OUT_DIM
ROT_DIM = HEAD_DIM // 4
ROPE_THETA = 500000.0
N_GROUPS = 4
EXPERTS_PER_GROUP = 8
N_EXPERTS = N_GROUPS * EXPERTS_PER_GROUP
NORM_EPS = 1e-6
ATTN_SPAN = 128

V7X_LANES = 128
V7X_VMEM_LIMIT_BYTES = 56 * 1024 * 1024

ROW_TILE = 512
CONV_COL_BLOCK = 256
ROUTE_TILE = 1024
MOE_BLOCK = 512
EXPERT_ROW_CHUNKS = 2
ATTN_Q_TILE = 512
ROUTER_LANE_OFFSET = 8
DMA_ROW_TILE = 1024
DMA_ISSUE_CHUNK = 128

F32 = jnp.float32
BF16 = jnp.bfloat16


def _params(n_axes):
    return pltpu.CompilerParams(dimension_semantics=("arbitrary",) * n_axes,
                                vmem_limit_bytes=V7X_VMEM_LIMIT_BYTES)


def _norm_mod(x, g, shift, scale):
    ms = jnp.mean(x * x, axis=-1, keepdims=True)
    y = x * lax.rsqrt(ms + NORM_EPS) * g
    return y * (1.0 + scale) + shift


def _pack_bf16_pair(a, b):
    return pltpu.pack_elementwise([a, b], packed_dtype=BF16)


def _unpack_bf16_pair(w):
    a = pltpu.unpack_elementwise(w, index=0, packed_dtype=BF16, unpacked_dtype=F32)
    b = pltpu.unpack_elementwise(w, index=1, packed_dtype=BF16, unpacked_dtype=F32)
    return a, b


def _store_row_groups(ref, x, first=0):
    rows, d = x.shape
    nj = d // (2 * V7X_LANES)
    for j in range(nj):
        lo = x[:, j * V7X_LANES:(j + 1) * V7X_LANES]
        hi = x[:, d // 2 + j * V7X_LANES:d // 2 + (j + 1) * V7X_LANES]
        ref[pl.ds(first * nj + j, rows, stride=nj), :] = _pack_bf16_pair(lo, hi)


def _load_row_groups(ref, rows, nj, first=0):
    los, his = [], []
    for j in range(nj):
        lo, hi = _unpack_bf16_pair(ref[pl.ds(first * nj + j, rows, stride=nj), :])
        los.append(lo)
        his.append(hi)
    return los + his


def _ffn_input_and_logits(x1, mod, g2_ref, wr_ref, br_ref, h2_ref, lg_ref):
    h2 = _norm_mod(x1, g2_ref[...], mod[3:4], mod[4:5])
    _store_row_groups(h2_ref, h2)
    lg_ref[...] = jnp.dot(h2.astype(BF16), wr_ref[...], preferred_element_type=F32) + br_ref[...]


def _ada_kernel(c_ref, w_ref, b_ref, o_ref):
    c = c_ref[...]
    ca = c * jax.nn.sigmoid(c)
    o_ref[0] = jnp.dot(ca, w_ref[0], preferred_element_type=F32,
                       precision=lax.Precision.HIGHEST) + b_ref[0]


def _ada_modulation(c, ada_w, ada_b):
    depth, d, n = ada_w.shape
    b = c.shape[0]
    tn = d
    return pl.pallas_call(
        _ada_kernel,
        out_shape=jax.ShapeDtypeStruct((depth, b, n), F32),
        grid=(depth, n // tn),
        in_specs=[pl.BlockSpec((b, d), lambda i, j: (0, 0)),
                  pl.BlockSpec((1, d, tn), lambda i, j: (i, 0, j)),
                  pl.BlockSpec((1, 1, tn), lambda i, j: (i, 0, j))],
        out_specs=pl.BlockSpec((1, b, tn), lambda i, j: (i, 0, j)),
        compiler_params=_params(2),
        name="ada_modulation",
    )(c, ada_w, ada_b.reshape(depth, 1, n))


def _conv_mixer_kernel(x_ref, mod_ref, g_ref, win_ref, cw_ref, wout_ref, g2_ref, wr_ref, br_ref,
                       x1_ref, h2_ref, lg_ref, carry_ref, v_ref, *, cb):
    ts, d = x_ref.shape[1], x_ref.shape[2]

    @pl.when(pl.program_id(1) == 0)
    def _():
        carry_ref[...] = jnp.zeros_like(carry_ref)

    x = x_ref[0]
    mod = mod_ref[0]
    h = _norm_mod(x, g_ref[...], mod[0:1], mod[1:2]).astype(BF16)
    row = lax.broadcasted_iota(jnp.int32, (ts, cb), 0)
    for j in range(d // cb):
        lo = j * cb
        bj = jnp.dot(h, win_ref[:, lo:lo + cb], preferred_element_type=F32)
        cj = jnp.dot(h, win_ref[:, d + lo:d + lo + cb], preferred_element_type=F32)
        uj = jnp.dot(h, win_ref[:, 2 * d + lo:2 * d + lo + cb], preferred_element_type=F32)
        z = cj * uj
        prev = carry_ref[:, lo:lo + cb]
        z1 = jnp.where(row == 0, prev[7:8], pltpu.roll(z, 1, 0))
        z2 = jnp.where(row == 0, prev[6:7], jnp.where(row == 1, prev[7:8], pltpu.roll(z, 2, 0)))
        zc = cw_ref[0:1, lo:lo + cb] * z2 + cw_ref[1:2, lo:lo + cb] * z1 + cw_ref[2:3, lo:lo + cb] * z
        carry_ref[:, lo:lo + cb] = z[ts - 8:ts]
        v_ref[:, lo:lo + cb] = (bj * zc).astype(BF16)
    y = jnp.dot(v_ref[...], wout_ref[...], preferred_element_type=F32)
    x1 = x + mod[2:3] * y
    x1_ref[0] = x1
    _ffn_input_and_logits(x1, mod, g2_ref, wr_ref, br_ref, h2_ref, lg_ref)


def _conv_mixer(x, mod, g_mix, w_in, conv_w, w_out, g_ffn, wr, br):
    b, s, d = x.shape
    ts = min(ROW_TILE, s)
    cb = min(CONV_COL_BLOCK, d)
    nst = s // ts
    nj = d // (2 * V7X_LANES)
    const = lambda bi, si: (0, 0)
    return pl.pallas_call(
        functools.partial(_conv_mixer_kernel, cb=cb),
        out_shape=(jax.ShapeDtypeStruct((b, s, d), F32),
                   jax.ShapeDtypeStruct((b * s * nj, V7X_LANES), jnp.uint32),
                   jax.ShapeDtypeStruct((b * s, V7X_LANES), F32)),
        grid=(b, nst),
        in_specs=[pl.BlockSpec((1, ts, d), lambda bi, si: (bi, si, 0)),
                  pl.BlockSpec((1, 6, d), lambda bi, si: (bi, 0, 0)),
                  pl.BlockSpec((1, d), const),
                  pl.BlockSpec((d, 3 * d), const),
                  pl.BlockSpec((CONV_WIDTH, d), const),
                  pl.BlockSpec((d, d), const),
                  pl.BlockSpec((1, d), const),
                  pl.BlockSpec((d, V7X_LANES), const),
                  pl.BlockSpec((1, V7X_LANES), const)],
        out_specs=(pl.BlockSpec((1, ts, d), lambda bi, si: (bi, si, 0)),
                   pl.BlockSpec((ts * nj, V7X_LANES), lambda bi, si: (bi * nst + si, 0)),
                   pl.BlockSpec((ts, V7X_LANES), lambda bi, si: (bi * nst + si, 0))),
        scratch_shapes=[pltpu.VMEM((8, d), F32), pltpu.VMEM((ts, d), BF16)],
        compiler_params=_params(2),
        name="conv_mixer",
    )(x, mod, g_mix, w_in, conv_w, w_out, g_ffn, wr, br)


def _route_kernel(lg_ref, tri_ref, plan_ref, cnt_ref, run_ref, pst_ref, *, blk):
    ph, i = pl.program_id(0), pl.program_id(1)
    tr = lg_ref.shape[0]
    lanes = lg_ref.shape[1]

    @pl.when((ph == 0) & (i == 0))
    def _():
        run_ref[...] = jnp.zeros_like(run_ref)

    lg = lg_ref[...]
    lane = lax.broadcasted_iota(jnp.int32, (tr, lanes), 1)
    big = jnp.int32(lanes)
    neg = jnp.float32(-jnp.inf)
    gmask = lane < N_GROUPS
    gl = jnp.where(gmask, lg, neg)
    mg = jnp.max(gl, axis=1, keepdims=True)
    gidx = jnp.min(jnp.where(gl == mg, lane, big), axis=1, keepdims=True)
    den_g = jnp.sum(jnp.where(gmask, jnp.exp(lg - mg), 0.0), axis=1, keepdims=True)
    grp_p = 1.0 / den_g
    lo = ROUTER_LANE_OFFSET + EXPERTS_PER_GROUP * gidx
    el = jnp.where((lane >= lo) & (lane < lo + EXPERTS_PER_GROUP), lg, neg)
    l1 = jnp.max(el, axis=1, keepdims=True)
    i1 = jnp.min(jnp.where(el == l1, lane, big), axis=1, keepdims=True)
    el2 = jnp.where(lane == i1, neg, el)
    l2 = jnp.max(el2, axis=1, keepdims=True)
    i2 = jnp.min(jnp.where(el2 == l2, lane, big), axis=1, keepdims=True)
    r = jnp.exp(l2 - l1)
    w1 = grp_p / (1.0 + r)
    w2 = grp_p * r / (1.0 + r)
    oh1 = (lane == i1 - ROUTER_LANE_OFFSET).astype(F32)
    oh2 = (lane == i2 - ROUTER_LANE_OFFSET).astype(F32)
    tot1 = jnp.sum(oh1, axis=0, keepdims=True)
    tot2 = jnp.sum(oh2, axis=0, keepdims=True)
    run = run_ref[0:1, :]

    @pl.when(ph == 0)
    def _():
        new_run = run + tot1 + tot2
        run_ref[...] = jnp.broadcast_to(new_run, run_ref.shape)

        @pl.when(i == pl.num_programs(1) - 1)
        def _():
            cnt = jnp.broadcast_to(new_run, run_ref.shape)
            padded = jnp.floor((cnt + (blk - 1)) / blk) * blk
            l8 = lax.broadcasted_iota(jnp.int32, cnt.shape, 1)
            csum = padded
            k = 1
            while k < N_EXPERTS:
                csum = csum + jnp.where(l8 >= k, pltpu.roll(csum, k, 1), 0.0)
                k *= 2
            pst_ref[...] = csum - padded
            run_ref[...] = jnp.zeros_like(run_ref)
            r8 = lax.broadcasted_iota(jnp.int32, cnt.shape, 0)
            cnt_ref[...] = jnp.where(r8 == 0, cnt, jnp.where(r8 == 1, csum, jnp.where(r8 == 2, padded, 0.0)))

    @pl.when(ph == 1)
    def _():
        oh = jnp.concatenate([oh1, oh2], axis=1).astype(BF16)
        before = jnp.dot(tri_ref[...], oh, preferred_element_type=F32)
        base = pst_ref[0:1, :] + run
        pos1 = jnp.sum(oh1 * (base + before[:, :lanes]), axis=1, keepdims=True)
        pos2 = jnp.sum(oh2 * (base + tot1 + before[:, lanes:]), axis=1, keepdims=True)
        run_ref[...] = jnp.broadcast_to(run + tot1 + tot2, run_ref.shape)
        plan_ref[...] = jnp.where(lane == 0, pos1, jnp.where(lane == 1, pos2,
                                  jnp.where(lane == 2, w1, jnp.where(lane == 3, w2, 0.0))))


def _route(logits):
    t, lanes = logits.shape
    tr = min(ROUTE_TILE, t)
    nt = t // tr
    tri = (jnp.arange(tr)[None, :] < jnp.arange(tr)[:, None]).astype(BF16)
    return pl.pallas_call(
        functools.partial(_route_kernel, blk=MOE_BLOCK),
        out_shape=(jax.ShapeDtypeStruct((t, lanes), F32),
                   jax.ShapeDtypeStruct((8, lanes), F32)),
        grid=(2, nt),
        in_specs=[pl.BlockSpec((tr, lanes), lambda ph, i: (i, 0)),
                  pl.BlockSpec((tr, tr), lambda ph, i: (0, 0))],
        out_specs=(pl.BlockSpec((tr, lanes), lambda ph, i: (i * ph, 0)),
                   pl.BlockSpec((8, lanes), lambda ph, i: (0, 0))),
        scratch_shapes=[pltpu.VMEM((8, lanes), F32), pltpu.VMEM((8, lanes), F32)],
        compiler_params=_params(2),
        name="route",
    )(logits, tri)


def _dispatch_kernel(last_ref, row_ref, h2_ref, xs_hbm, zero_buf, sem, zero_sem, *, nj):
    ts = row_ref.shape[1]
    blk_rows = zero_buf.shape[0]

    @pl.when(pl.program_id(0) == 0)
    def _():
        z = jnp.zeros(zero_buf.shape, F32)
        zero_buf[...] = _pack_bf16_pair(z, z)

        def blank(e):
            dst = xs_hbm.at[pl.ds(pl.multiple_of(last_ref[e], nj), blk_rows)]
            return pltpu.make_async_copy(zero_buf, dst, zero_sem)

        def zstart(e, carry):
            @pl.when(last_ref[e] >= 0)
            def _():
                blank(e).start()
            return carry

        def zwait(e, carry):
            @pl.when(last_ref[e] >= 0)
            def _():
                blank(e).wait()
            return carry

        def tail(bi):
            dst = xs_hbm.at[pl.ds(pl.multiple_of(bi * blk_rows, blk_rows), blk_rows)]
            return pltpu.make_async_copy(zero_buf, dst, zero_sem)

        def tstart(bi, carry):
            tail(bi).start()
            return carry

        def twait(bi, carry):
            tail(bi).wait()
            return carry

        n_blocks = xs_hbm.shape[0] // blk_rows
        lax.fori_loop(0, N_EXPERTS, zstart, 0)
        lax.fori_loop(last_ref[N_EXPERTS], n_blocks, tstart, 0)
        lax.fori_loop(0, N_EXPERTS, zwait, 0)
        lax.fori_loop(last_ref[N_EXPERTS], n_blocks, twait, 0)

    chunk = min(DMA_ISSUE_CHUNK, ts)

    def start(c, carry):
        t0 = pl.multiple_of(c * chunk, chunk)
        for u in range(chunk):
            src = h2_ref.at[pl.ds(pl.multiple_of((t0 + u) * nj, nj), nj)]
            for k in range(2):
                pltpu.make_async_copy(src, xs_hbm.at[pl.ds(pl.multiple_of(row_ref[k, t0 + u], nj), nj)],
                                      sem).start(priority=k)
        return carry

    lax.fori_loop(0, ts // chunk, start, 0)
    for _ in range(2):
        pltpu.make_async_copy(h2_ref, xs_hbm.at[pl.ds(0, ts * nj)], sem).wait()


def _dispatch(last_row, row_t, h2, n_rows, nj):
    t = row_t.shape[1]
    ts = min(DMA_ROW_TILE, t)
    grid_spec = pltpu.PrefetchScalarGridSpec(
        num_scalar_prefetch=1,
        grid=(t // ts,),
        in_specs=[pl.BlockSpec((2, ts), lambda i, last: (0, i), memory_space=pltpu.SMEM),
                  pl.BlockSpec((ts * nj, V7X_LANES), lambda i, last: (i, 0))],
        out_specs=pl.BlockSpec(memory_space=pl.ANY),
        scratch_shapes=[pltpu.VMEM((MOE_BLOCK * nj, V7X_LANES), h2.dtype),
                        pltpu.SemaphoreType.DMA(()), pltpu.SemaphoreType.DMA(())])
    return pl.pallas_call(
        functools.partial(_dispatch_kernel, nj=nj),
        out_shape=jax.ShapeDtypeStruct((n_rows * nj, V7X_LANES), h2.dtype),
        grid_spec=grid_spec,
        compiler_params=_params(1),
        name="dispatch",
    )(last_row, row_t, h2)


def _experts_kernel(be_ref, nact_ref, xs_ref, wg_ref, wu_ref, wd_ref, ys_ref, wgb, wub, wdb, *, nj):
    i = pl.program_id(0)
    blk = xs_ref.shape[0] // nj

    @pl.when(i < nact_ref[0])
    def _():
        prev = be_ref[jnp.maximum(i - 1, 0)]

        @pl.when((i == 0) | (be_ref[i] != prev))
        def _():
            wgb[...] = wg_ref[0, 0].astype(BF16)
            wub[...] = wu_ref[0, 0].astype(BF16)
            wdb[...] = wd_ref[0, 0].astype(BF16)

        nchunk = EXPERT_ROW_CHUNKS if blk % (8 * EXPERT_ROW_CHUNKS) == 0 else 1
        rows = blk // nchunk
        xs = [jnp.concatenate([p.astype(BF16) for p in _load_row_groups(xs_ref, rows, nj, c * rows)], axis=1)
              for c in range(nchunk)]
        gs = [jnp.dot(x, wgb[...], preferred_element_type=F32) for x in xs]
        us = [jnp.dot(x, wub[...], preferred_element_type=F32) for x in xs]
        acts = [(g * jax.nn.sigmoid(g) * u).astype(BF16) for g, u in zip(gs, us)]
        ys = [jnp.dot(a, wdb[...], preferred_element_type=F32) for a in acts]
        for c, y in enumerate(ys):
            _store_row_groups(ys_ref, y, c * rows)

    @pl.when(i >= nact_ref[0])
    def _():
        z = jnp.zeros(ys_ref.shape, F32)
        ys_ref[...] = _pack_bf16_pair(z, z)


def _experts(block_e, n_active, xs, w_gate, w_up, w_down, layer):
    _, _, d, de = w_gate.shape
    nj = d // (2 * V7X_LANES)
    n_blocks = xs.shape[0] // (MOE_BLOCK * nj)
    grid_spec = pltpu.PrefetchScalarGridSpec(
        num_scalar_prefetch=2,
        grid=(n_blocks,),
        in_specs=[pl.BlockSpec((MOE_BLOCK * nj, V7X_LANES), lambda i, be, na: (i, 0)),
                  pl.BlockSpec((1, 1, d, de), lambda i, be, na: (layer, be[i], 0, 0)),
                  pl.BlockSpec((1, 1, d, de), lambda i, be, na: (layer, be[i], 0, 0)),
                  pl.BlockSpec((1, 1, de, d), lambda i, be, na: (layer, be[i], 0, 0))],
        out_specs=pl.BlockSpec((MOE_BLOCK * nj, V7X_LANES), lambda i, be, na: (i, 0)),
        scratch_shapes=[pltpu.VMEM((d, de), BF16), pltpu.VMEM((d, de), BF16),
                        pltpu.VMEM((de, d), BF16)])
    return pl.pallas_call(
        functools.partial(_experts_kernel, nj=nj),
        out_shape=jax.ShapeDtypeStruct(xs.shape, jnp.uint32),
        grid_spec=grid_spec,
        compiler_params=_params(1),
        name="experts",
    )(block_e, n_active, xs, w_gate, w_up, w_down)


def _start_row_gathers(row_ref, ys_hbm, buf, sem, ts, nj):
    chunk = min(DMA_ISSUE_CHUNK, ts)

    def start(c, carry):
        t0 = pl.multiple_of(c * chunk, chunk)
        for u in range(chunk):
            dst = pl.ds(pl.multiple_of((t0 + u) * nj, nj), nj)
            for k in range(2):
                pltpu.make_async_copy(ys_hbm.at[pl.ds(pl.multiple_of(row_ref[k, t0 + u], nj), nj)],
                                      buf.at[k, dst], sem).start(priority=k)
        return carry

    lax.fori_loop(0, ts // chunk, start, 0)


def _wait_row_gathers(ys_hbm, buf, sem, ts, nj):
    for k in range(2):
        pltpu.make_async_copy(ys_hbm.at[pl.ds(0, ts * nj)], buf.at[k], sem).wait()


def _weighted_expert_rows(buf, plan_ref, ts, nj):
    w1 = plan_ref[:, 2:3]
    w2 = plan_ref[:, 3:4]
    p1 = _load_row_groups(buf.at[0], ts, nj)
    p2 = _load_row_groups(buf.at[1], ts, nj)
    return jnp.concatenate([w1 * a + w2 * b for a, b in zip(p1, p2)], axis=1)


def _gathered_expert_rows(step, n_steps, rcur_ref, rnext_ref, plan_ref, ys_hbm, buf, sems, ts, nj):
    slot = lax.rem(step, 2)

    @pl.when(step == 0)
    def _():
        _start_row_gathers(rcur_ref, ys_hbm, buf.at[0], sems.at[0], ts, nj)

    @pl.when(step + 1 < n_steps)
    def _():
        _start_row_gathers(rnext_ref, ys_hbm, buf.at[1 - slot], sems.at[1 - slot], ts, nj)

    _wait_row_gathers(ys_hbm, buf.at[slot], sems.at[slot], ts, nj)
    return _weighted_expert_rows(buf.at[slot], plan_ref, ts, nj)


def _combine_kernel(rcur_ref, rnext_ref, x_ref, plan_ref, mod_ref, gfin_ref, ys_hbm, out_ref, buf, sems, *, nj):
    ts = x_ref.shape[1]
    moe = _gathered_expert_rows(pl.program_id(0), pl.num_programs(0), rcur_ref, rnext_ref, plan_ref,
                                ys_hbm, buf, sems, ts, nj)
    x2 = x_ref[0] + mod_ref[0][5:6] * moe
    ms = jnp.mean(x2 * x2, axis=-1, keepdims=True)
    out_ref[0] = x2 * lax.rsqrt(ms + NORM_EPS) * gfin_ref[...]


def _combine_final(row_t, x1, plan, mod, g_final, ys):
    b, s, d = x1.shape
    ts = min(DMA_ROW_TILE, s)
    nst = s // ts
    n_tiles = b * nst
    nj = d // (2 * V7X_LANES)
    return pl.pallas_call(
        functools.partial(_combine_kernel, nj=nj),
        out_shape=jax.ShapeDtypeStruct((b, s, d), F32),
        grid=(n_tiles,),
        in_specs=[pl.BlockSpec((2, ts), lambda i: (0, i), memory_space=pltpu.SMEM),
                  pl.BlockSpec((2, ts), lambda i: (0, jnp.minimum(i + 1, n_tiles - 1)),
                               memory_space=pltpu.SMEM),
                  pl.BlockSpec((1, ts, d), lambda i: (i // nst, i % nst, 0)),
                  pl.BlockSpec((ts, plan.shape[1]), lambda i: (i, 0)),
                  pl.BlockSpec((1, 6, d), lambda i: (i // nst, 0, 0)),
                  pl.BlockSpec((1, d), lambda i: (0, 0)),
                  pl.BlockSpec(memory_space=pl.ANY)],
        out_specs=pl.BlockSpec((1, ts, d), lambda i: (i // nst, i % nst, 0)),
        scratch_shapes=[pltpu.VMEM((2, 2, ts * nj, V7X_LANES), jnp.uint32),
                        pltpu.SemaphoreType.DMA((2,))],
        compiler_params=_params(1),
        name="combine_final",
    )(row_t, row_t, x1, plan, mod, g_final, ys)


def _moe_experts(h2, logits, w_gate, w_up, w_down, *, layer):
    t = logits.shape[0]
    nj = h2.shape[0] // t
    plan, cnt = _route(logits)
    n_blocks = -(-(t * 2) // MOE_BLOCK) + N_EXPERTS
    seg_end, seg_rows = cnt[1, :N_EXPERTS], cnt[2, :N_EXPERTS]
    pends = (seg_end * (1.0 / MOE_BLOCK)).astype(jnp.int32)
    block_e = jnp.minimum(jnp.sum(pends[None, :] <= jnp.arange(n_blocks)[:, None], axis=1),
                          N_EXPERTS - 1).astype(jnp.int32)
    n_active = pends[-1:]
    last_row = jnp.where(seg_rows > 0, (seg_end - MOE_BLOCK) * nj, -1.0).astype(jnp.int32)
    last_row = jnp.concatenate([last_row, n_active])
    row_t = (plan[:, 0:2] * nj).astype(jnp.int32).T
    xs = _dispatch(last_row, row_t, h2, n_blocks * MOE_BLOCK, nj)
    ys = _experts(block_e, n_active, xs, w_gate, w_up, w_down, layer)
    return row_t, plan, ys


def _qkv_kernel(rcur_ref, rnext_ref, x_ref, plan_ref, mod0_ref, mod_ref, g_ref, w_ref, *refs, nj):
    tabs, ys_hbm, x2_ref, (o0_ref, o1_ref, o2_ref) = refs[:9], refs[9], refs[10], refs[11:14]
    scr, buf, sems = refs[14:17]
    ts, d = x_ref.shape[1], x_ref.shape[2]
    step = pl.program_id(0) * pl.num_programs(1) + pl.program_id(1)
    n_steps = pl.num_programs(0) * pl.num_programs(1)
    moe = _gathered_expert_rows(step, n_steps, rcur_ref, rnext_ref, plan_ref, ys_hbm, buf, sems, ts, nj)
    x2 = x_ref[0] + mod0_ref[0][5:6] * moe
    x2_ref[0] = x2
    mod = mod_ref[0]
    hn = _norm_mod(x2, g_ref[...], mod[0:1], mod[1:2])
    wd = ATTN_OUT_DIM
    rep = wd // V7X_LANES
    nc = d // V7X_LANES
    for g, o_ref in enumerate((o0_ref, o1_ref, o2_ref)):
        dil = DILATED_GROUPS[g][1]
        n = ts // dil
        if dil == 1:
            h = hn.astype(BF16)
        else:
            if g == 1:
                for cc in range(nc):
                    scr[cc] = hn[:, cc * V7X_LANES:(cc + 1) * V7X_LANES]
            h = jnp.concatenate(
                [jnp.concatenate([scr[cc, pl.ds(r, n, stride=dil), :] for cc in range(nc)], axis=1)
                 for r in range(dil)], axis=0).astype(BF16)
        cos, sa, sb = (jnp.tile(t[...], (1, rep)) for t in tabs[3 * g:3 * g + 3])
        for j in range(3):
            slab = g * 3 + j
            acc = jnp.dot(h, w_ref[:, slab * wd:(slab + 1) * wd], preferred_element_type=F32)
            if j != 2:
                acc = (acc * cos + pltpu.roll(acc, ROT_DIM // 2, 1) * sa
                       + pltpu.roll(acc, wd - ROT_DIM // 2, 1) * sb)
            acc = acc.astype(BF16)
            for r in range(dil):
                o_ref[0, r, :, j * wd:(j + 1) * wd] = acc[r * n:(r + 1) * n]


def _rope_tables(s, ts):
    half = ROT_DIM // 2
    pos = jnp.arange(s, dtype=F32)
    inv_freq = jnp.power(jnp.float32(ROPE_THETA), -jnp.arange(0, ROT_DIM, 2, dtype=F32) / ROT_DIM)
    ang = pos[:, None] * inv_freq[None, :]
    cos, sin = jnp.cos(ang), jnp.sin(ang)
    m = jnp.arange(V7X_LANES) % HEAD_DIM
    cos_l = jnp.where(m[None, :] < ROT_DIM, cos[:, m % half], 1.0)
    sa_l = jnp.where((m[None, :] >= half) & (m[None, :] < ROT_DIM), sin[:, m % half], 0.0)
    sb_l = jnp.where(m[None, :] < half, -sin[:, m % half], 0.0)
    tabs = []
    for _, dil in DILATED_GROUPS:
        for t in (cos_l, sa_l, sb_l):
            t = t.astype(F32).reshape(s // ts, ts // dil, dil, V7X_LANES)
            tabs.append(jnp.transpose(t, (0, 2, 1, 3)).reshape(s, V7X_LANES))
    return tabs


def _combine_qkv_proj(row_t, x1, plan, ys, mod0, mod, g_mix, w_in):
    b, s, d = x1.shape
    ts = min(ROW_TILE, s)
    nst = s // ts
    nj = d // (2 * V7X_LANES)
    const = lambda si, bi: (0, 0)
    tab = pl.BlockSpec((ts, V7X_LANES), lambda si, bi: (si, 0))
    wd3 = 3 * ATTN_OUT_DIM

    def next_tile(si, bi):
        nb = lax.rem(bi + 1, b)
        ns = jnp.minimum(si + lax.div(bi + 1, b), nst - 1)
        return (0, nb * nst + ns)

    return pl.pallas_call(
        functools.partial(_qkv_kernel, nj=nj),
        out_shape=(jax.ShapeDtypeStruct((b, s, d), F32),)
                  + tuple(jax.ShapeDtypeStruct((b, dil, s // dil, wd3), BF16) for _, dil in DILATED_GROUPS),
        grid=(nst, b),
        in_specs=[pl.BlockSpec((2, ts), lambda si, bi: (0, bi * nst + si), memory_space=pltpu.SMEM),
                  pl.BlockSpec((2, ts), next_tile, memory_space=pltpu.SMEM),
                  pl.BlockSpec((1, ts, d), lambda si, bi: (bi, si, 0)),
                  pl.BlockSpec((ts, plan.shape[1]), lambda si, bi: (bi * nst + si, 0)),
                  pl.BlockSpec((1, 6, d), lambda si, bi: (bi, 0, 0)),
                  pl.BlockSpec((1, 6, d), lambda si, bi: (bi, 0, 0)),
                  pl.BlockSpec((1, d), const),
                  pl.BlockSpec((d, ATTN_IN_DIM), const)] + [tab] * 9
                 + [pl.BlockSpec(memory_space=pl.ANY)],
        out_specs=(pl.BlockSpec((1, ts, d), lambda si, bi: (bi, si, 0)),)
                  + tuple(pl.BlockSpec((1, dil, ts // dil, wd3), lambda si, bi: (bi, 0, si, 0))
                          for _, dil in DILATED_GROUPS),
        scratch_shapes=[pltpu.VMEM((d // V7X_LANES, ts, V7X_LANES), F32),
                        pltpu.VMEM((2, 2, ts * nj, V7X_LANES), jnp.uint32),
                        pltpu.SemaphoreType.DMA((2,))],
        compiler_params=_params(2),
        name="combine_qkv_proj",
    )(row_t, row_t, x1, plan, mod0, mod, g_mix, w_in, *_rope_tables(s, ts), ys)


def _attn_kernel(q_ref, kc_ref, kp_ref, vc_ref, vp_ref, o_ref, lse_ref):
    tq = q_ref.shape[2]
    sp = ATTN_SPAN
    nl = lse_ref.shape[3]
    first_tile = pl.program_id(2) == 0
    kext = jnp.concatenate([kp_ref[0, 0], kc_ref[0, 0]], axis=0)
    vext = jnp.concatenate([vp_ref[0, 0], vc_ref[0, 0]], axis=0)
    q_t = (q_ref[0, 0].astype(F32) * (HEAD_DIM ** -0.5)).T.astype(BF16)
    v_t = vext.astype(F32).T.astype(BF16)
    kj = lax.broadcasted_iota(jnp.int32, (2 * sp, sp), 0)
    qi = lax.broadcasted_iota(jnp.int32, (2 * sp, sp), 1)
    neg = jnp.float32(-jnp.inf)
    band = jnp.where(kj >= qi, jnp.where(kj <= qi + sp, 0.0, neg), neg)
    zeros_half = jnp.zeros((HEAD_DIM, sp), BF16)
    head_row = lax.broadcasted_iota(jnp.int32, (ATTN_HEADS, sp), 0)
    out_cols, lse_cols = [], []
    for i in range(tq // sp):
        if i == 0:
            bias = jnp.where(first_tile & (kj < sp), neg, band)
        else:
            bias = band
        scores = []
        for h in range(ATTN_HEADS):
            pair = h // 2
            k_pair = kext[i * sp:(i + 2) * sp, pair * 2 * HEAD_DIM:(pair + 1) * 2 * HEAD_DIM]
            q_h = q_t[h * HEAD_DIM:(h + 1) * HEAD_DIM, i * sp:(i + 1) * sp]
            q_m = jnp.concatenate([q_h, zeros_half] if h % 2 == 0 else [zeros_half, q_h], axis=0)
            scores.append(jnp.dot(k_pair, q_m, preferred_element_type=F32) + bias)
        probs, dens = [], []
        lse_t = jnp.zeros((ATTN_HEADS, sp), F32)
        for h, sc in enumerate(scores):
            m = jnp.max(sc, axis=0, keepdims=True)
            p = jnp.exp(sc - m)
            den = jnp.sum(p, axis=0, keepdims=True)
            probs.append(p.astype(BF16))
            dens.append(den)
            lse_t = jnp.where(head_row == h, m + jnp.log(den), lse_t)
        o_rows = []
        for h in range(ATTN_HEADS):
            v_h = v_t[h * HEAD_DIM:(h + 1) * HEAD_DIM, i * sp:(i + 2) * sp]
            o_rows.append(jnp.dot(v_h, probs[h], preferred_element_type=F32) / dens[h])
        out_cols.append(jnp.concatenate(o_rows, axis=0))
        lse_cols.append(jnp.concatenate([lse_t, jnp.zeros((nl - ATTN_HEADS, sp), F32)], axis=0))
    o_ref[0, 0] = jnp.concatenate(out_cols, axis=1).T.astype(BF16)
    lse_ref[0, 0] = jnp.concatenate(lse_cols, axis=1).T


def _dilated_attention(qkv_g):
    b, dil, L, _ = qkv_g.shape
    tq = min(ATTN_Q_TILE, L)
    sub = tq // ATTN_SPAN
    wd = ATTN_OUT_DIM

    def cur(j):
        return pl.BlockSpec((1, 1, tq, wd), lambda bi, r, lb: (bi, r, lb, j))

    def prev(j):
        return pl.BlockSpec((1, 1, ATTN_SPAN, wd),
                            lambda bi, r, lb: (bi, r, jnp.maximum(lb * sub - 1, 0), j))

    return pl.pallas_call(
        _attn_kernel,
        out_shape=(jax.ShapeDtypeStruct((b, dil, L, wd), BF16),
                   jax.ShapeDtypeStruct((b, dil, L, V7X_LANES), F32)),
        grid=(b, dil, L // tq),
        in_specs=[cur(0), cur(1), prev(1), cur(2), prev(2)],
        out_specs=(pl.BlockSpec((1, 1, tq, wd), lambda bi, r, lb: (bi, r, lb, 0)),
                   pl.BlockSpec((1, 1, tq, V7X_LANES), lambda bi, r, lb: (bi, r, lb, 0))),
        compiler_params=_params(3),
        name=f"dilated_attention_{dil}",
    )(qkv_g, qkv_g, qkv_g, qkv_g, qkv_g)


def _attn_out_kernel(o0_ref, o1_ref, o2_ref, l0_ref, l1_ref, l2_ref, x_ref, mod_ref, wout_ref, e_ref,
                     g2_ref, wr_ref, br_ref, x1_ref, h2_ref, lg_ref, o_scr, l_scr):
    ts = x_ref.shape[1]
    nc = ATTN_OUT_DIM // V7X_LANES
    for g, (o_ref, l_ref) in enumerate(((o0_ref, l0_ref), (o1_ref, l1_ref), (o2_ref, l2_ref))):
        dil = DILATED_GROUPS[g][1]
        for r in range(dil):
            rows = pl.ds(r, ts // dil, stride=dil) if dil > 1 else slice(None)
            o_r = o_ref[0, r].astype(F32)
            for cc in range(nc):
                o_scr[g, cc, rows, :] = o_r[:, cc * V7X_LANES:(cc + 1) * V7X_LANES]
            l_scr[g, rows, :] = l_ref[0, r]
    l0, l1, l2 = l_scr[0], l_scr[1], l_scr[2]
    m = jnp.maximum(jnp.maximum(l0, l1), l2)
    es = (jnp.exp(l0 - m), jnp.exp(l1 - m), jnp.exp(l2 - m))
    den = es[0] + es[1] + es[2]
    merged = jnp.zeros((ts, ATTN_OUT_DIM), F32)
    for g in range(N_DIL_GROUPS):
        w = es[g] / den
        w_hi = w.astype(BF16)
        w_lo = (w - w_hi.astype(F32)).astype(BF16)
        wfull = (jnp.dot(w_hi, e_ref[...], preferred_element_type=F32)
                 + jnp.dot(w_lo, e_ref[...], preferred_element_type=F32))
        merged = merged + wfull * jnp.concatenate([o_scr[g, cc] for cc in range(nc)], axis=1)
    y = jnp.dot(merged.astype(BF16), wout_ref[...], preferred_element_type=F32)
    mod = mod_ref[0]
    x1 = x_ref[0] + mod[2:3] * y
    x1_ref[0] = x1
    _ffn_input_and_logits(x1, mod, g2_ref, wr_ref, br_ref, h2_ref, lg_ref)


def _attn_out(outs, lses, x, mod, w_out, g_ffn, wr, br):
    b, s, d = x.shape
    ts = min(ROW_TILE, s)
    nst = s // ts
    nj = d // (2 * V7X_LANES)
    const = lambda bi, si: (0, 0)
    tok = lambda bi, si: (bi * nst + si, 0)
    stream = lambda bi, si: (bi, 0, si, 0)
    expand = (jnp.arange(V7X_LANES)[:, None] == jnp.arange(ATTN_OUT_DIM)[None, :] // HEAD_DIM).astype(BF16)
    return pl.pallas_call(
        _attn_out_kernel,
        out_shape=(jax.ShapeDtypeStruct((b, s, d), F32),
                   jax.ShapeDtypeStruct((b * s * nj, V7X_LANES), jnp.uint32),
                   jax.ShapeDtypeStruct((b * s, V7X_LANES), F32)),
        grid=(b, nst),
        in_specs=[pl.BlockSpec((1, dil, ts // dil, ATTN_OUT_DIM), stream) for _, dil in DILATED_GROUPS]
                 + [pl.BlockSpec((1, dil, ts // dil, V7X_LANES), stream) for _, dil in DILATED_GROUPS]
                 + [pl.BlockSpec((1, ts, d), lambda bi, si: (bi, si, 0)),
                    pl.BlockSpec((1, 6, d), lambda bi, si: (bi, 0, 0)),
                    pl.BlockSpec((ATTN_OUT_DIM, d), const),
                    pl.BlockSpec((V7X_LANES, ATTN_OUT_DIM), const),
                    pl.BlockSpec((1, d), const),
                    pl.BlockSpec((d, V7X_LANES), const),
                    pl.BlockSpec((1, V7X_LANES), const)],
        out_specs=(pl.BlockSpec((1, ts, d), lambda bi, si: (bi, si, 0)),
                   pl.BlockSpec((ts * nj, V7X_LANES), tok),
                   pl.BlockSpec((ts, V7X_LANES), tok)),
        scratch_shapes=[pltpu.VMEM((N_DIL_GROUPS, ATTN_OUT_DIM // V7X_LANES, ts, V7X_LANES), F32),
                        pltpu.VMEM((N_DIL_GROUPS, ts, V7X_LANES), F32)],
        compiler_params=_params(2),
        name="attn_out",
    )(*outs, *lses, x, mod, w_out, expand, g_ffn, wr, br)


def _router_params(w_grp, b_grp, w_exp, b_exp):
    d = w_grp.shape[0]
    w_e = jnp.transpose(w_exp, (1, 0, 2)).reshape(d, N_EXPERTS)
    pad1 = ROUTER_LANE_OFFSET - N_GROUPS
    pad2 = V7X_LANES - ROUTER_LANE_OFFSET - N_EXPERTS
    wr = jnp.concatenate([w_grp, jnp.zeros((d, pad1), F32), w_e, jnp.zeros((d, pad2), F32)], axis=1)
    br = jnp.concatenate([b_grp, jnp.zeros((pad1,), F32), b_exp.reshape(-1), jnp.zeros((pad2,), F32)])
    return wr.astype(BF16), br.reshape(1, V7X_LANES).astype(F32)


def kernel(x, c, norm_mix_g, norm_ffn_g, ada_w, ada_b, conv_in_w, conv_w, conv_out_w, attn_in_w,
           attn_out_w, router_grp_w, router_grp_b, router_exp_w, router_exp_b, exp_gate_w, exp_up_w,
           exp_down_w, final_norm_g):
    b, s, d = x.shape
    depth = ada_w.shape[0]
    assert depth == 2 and s % (DILATED_GROUPS[-1][1] * ATTN_SPAN) == 0 and d % (2 * V7X_LANES) == 0
    mod = _ada_modulation(c, ada_w, ada_b).reshape(depth, b, 6, d)
    g_fin = final_norm_g.reshape(1, d)

    wr, br = _router_params(router_grp_w[0], router_grp_b[0], router_exp_w[0], router_exp_b[0])
    x1, h2, logits = _conv_mixer(x, mod[0], norm_mix_g[0:1], conv_in_w[0].astype(BF16), conv_w[0],
                                 conv_out_w[0].astype(BF16), norm_ffn_g[0:1], wr, br)
    row_t, plan, ys = _moe_experts(h2, logits, exp_gate_w, exp_up_w, exp_down_w, layer=0)

    wr, br = _router_params(router_grp_w[1], router_grp_b[1], router_exp_w[1], router_exp_b[1])
    x2, *qkv = _combine_qkv_proj(row_t, x1, plan, ys, mod[0], mod[1], norm_mix_g[1:2],
                                 attn_in_w[0].astype(BF16))
    outs, lses = zip(*[_dilated_attention(qkv_g) for qkv_g in qkv])
    x3, h2, logits = _attn_out(outs, lses, x2, mod[1], attn_out_w[0].astype(BF16), norm_ffn_g[1:2],
                               wr, br)
    row_t, plan, ys = _moe_experts(h2, logits, exp_gate_w, exp_up_w, exp_down_w, layer=1)
    return _combine_final(row_t, x3, plan, mod[1], g_fin, ys)
```

```python
import functools

import jax
import jax.numpy as jnp
from jax import lax
from jax.experimental import pallas as pl
from jax.experimental.pallas import tpu as pltpu

CONV_WIDTH = 3
DILATED_GROUPS = ((128, 1), (512, 4), (2048, 16))
N_DIL_GROUPS = len(DILATED_GROUPS)
ATTN_HEADS = 8
HEAD_DIM = 64
ATTN_OUT_DIM = ATTN_HEADS * HEAD_DIM
ATTN_IN_DIM = N_DIL_GROUPS * 3 * ATTN_OUT_DIM
ROT_DIM = HEAD_DIM // 4
ROPE_THETA = 500000.0
N_GROUPS = 4
EXPERTS_PER_GROUP = 8
N_EXPERTS = N_GROUPS * EXPERTS_PER_GROUP
NORM_EPS = 1e-6
ATTN_SPAN = 128

V7X_LANES = 128
V7X_VMEM_LIMIT_BYTES = 56 * 1024 * 1024

ROW_TILE = 512
CONV_COL_BLOCK = 256
ROUTE_TILE = 1024
MOE_BLOCK = 512
EXPERT_ROW_CHUNKS = 2
ATTN_Q_TILE = 512
ROUTER_LANE_OFFSET = 8
DMA_ROW_TILE = 1024
DMA_ISSUE_CHUNK = 128

F32 = jnp.float32
BF16 = jnp.bfloat16


def _params(n_axes):
    return pltpu.CompilerParams(dimension_semantics=("arbitrary",) * n_axes,
                                vmem_limit_bytes=V7X_VMEM_LIMIT_BYTES)


def _norm_mod(x, g, shift, scale):
    ms = jnp.mean(x * x, axis=-1, keepdims=True)
    y = x * lax.rsqrt(ms + NORM_EPS) * g
    return y * (1.0 + scale) + shift


def _pack_bf16_pair(a, b):
    return pltpu.pack_elementwise([a, b], packed_dtype=BF16)


def _unpack_bf16_pair(w):
    a = pltpu.unpack_elementwise(w, index=0, packed_dtype=BF16, unpacked_dtype=F32)
    b = pltpu.unpack_elementwise(w, index=1, packed_dtype=BF16, unpacked_dtype=F32)
    return a, b


def _store_row_groups(ref, x, first=0):
    rows, d = x.shape
    nj = d // (2 * V7X_LANES)
    for j in range(nj):
        lo = x[:, j * V7X_LANES:(j + 1) * V7X_LANES]
        hi = x[:, d // 2 + j * V7X_LANES:d // 2 + (j + 1) * V7X_LANES]
        ref[pl.ds(first * nj + j, rows, stride=nj), :] = _pack_bf16_pair(lo, hi)


def _load_row_groups(ref, rows, nj, first=0):
    los, his = [], []
    for j in range(nj):
        lo, hi = _unpack_bf16_pair(ref[pl.ds(first * nj + j, rows, stride=nj), :])
        los.append(lo)
        his.append(hi)
    return los + his


def _ffn_input_and_logits(x1, mod, g2_ref, wr_ref, br_ref, h2_ref, lg_ref):
    h2 = _norm_mod(x1, g2_ref[...], mod[3:4], mod[4:5])
    _store_row_groups(h2_ref, h2)
    lg_ref[...] = jnp.dot(h2.astype(BF16), wr_ref[...], preferred_element_type=F32) + br_ref[...]


def _ada_kernel(c_ref, w_ref, b_ref, o_ref):
    c = c_ref[...]
    ca = c * jax.nn.sigmoid(c)
    o_ref[0] = jnp.dot(ca, w_ref[0], preferred_element_type=F32,
                       precision=lax.Precision.HIGHEST) + b_ref[0]


def _ada_modulation(c, ada_w, ada_b):
    depth, d, n = ada_w.shape
    b = c.shape[0]
    tn = d
    return pl.pallas_call(
        _ada_kernel,
        out_shape=jax.ShapeDtypeStruct((depth, b, n), F32),
        grid=(depth, n // tn),
        in_specs=[pl.BlockSpec((b, d), lambda i, j: (0, 0)),
                  pl.BlockSpec((1, d, tn), lambda i, j: (i, 0, j)),
                  pl.BlockSpec((1, 1, tn), lambda i, j: (i, 0, j))],
        out_specs=pl.BlockSpec((1, b, tn), lambda i, j: (i, 0, j)),
        compiler_params=_params(2),
        name="ada_modulation",
    )(c, ada_w, ada_b.reshape(depth, 1, n))


def _conv_mixer_kernel(x_ref, mod_ref, g_ref, win_ref, cw_ref, wout_ref, g2_ref, wr_ref, br_ref,
                       x1_ref, h2_ref, lg_ref, carry_ref, v_ref, *, cb):
    ts, d = x_ref.shape[1], x_ref.shape[2]

    @pl.when(pl.program_id(1) == 0)
    def _():
        carry_ref[...] = jnp.zeros_like(carry_ref)

    x = x_ref[0]
    mod = mod_ref[0]
    h = _norm_mod(x, g_ref[...], mod[0:1], mod[1:2]).astype(BF16)
    row = lax.broadcasted_iota(jnp.int32, (ts, cb), 0)
    for j in range(d // cb):
        lo = j * cb
        bj = jnp.dot(h, win_ref[:, lo:lo + cb], preferred_element_type=F32)
        cj = jnp.dot(h, win_ref[:, d + lo:d + lo + cb], preferred_element_type=F32)
        uj = jnp.dot(h, win_ref[:, 2 * d + lo:2 * d + lo + cb], preferred_element_type=F32)
        z = cj * uj
        prev = carry_ref[:, lo:lo + cb]
        z1 = jnp.where(row == 0, prev[7:8], pltpu.roll(z, 1, 0))
        z2 = jnp.where(row == 0, prev[6:7], jnp.where(row == 1, prev[7:8], pltpu.roll(z, 2, 0)))
        zc = cw_ref[0:1, lo:lo + cb] * z2 + cw_ref[1:2, lo:lo + cb] * z1 + cw_ref[2:3, lo:lo + cb] * z
        carry_ref[:, lo:lo + cb] = z[ts - 8:ts]
        v_ref[:, lo:lo + cb] = (bj * zc).astype(BF16)
    y = jnp.dot(v_ref[...], wout_ref[...], preferred_element_type=F32)
    x1 = x + mod[2:3] * y
    x1_ref[0] = x1
    _ffn_input_and_logits(x1, mod, g2_ref, wr_ref, br_ref, h2_ref, lg_ref)


def _conv_mixer(x, mod, g_mix, w_in, conv_w, w_out, g_ffn, wr, br):
    b, s, d = x.shape
    ts = min(ROW_TILE, s)
    cb = min(CONV_COL_BLOCK, d)
    nst = s // ts
    nj = d // (2 * V7X_LANES)
    const = lambda bi, si: (0, 0)
    return pl.pallas_call(
        functools.partial(_conv_mixer_kernel, cb=cb),
        out_shape=(jax.ShapeDtypeStruct((b, s, d), F32),
                   jax.ShapeDtypeStruct((b * s * nj, V7X_LANES), jnp.uint32),
                   jax.ShapeDtypeStruct((b * s, V7X_LANES), F32)),
        grid=(b, nst),
        in_specs=[pl.BlockSpec((1, ts, d), lambda bi, si: (bi, si, 0)),
                  pl.BlockSpec((1, 6, d), lambda bi, si: (bi, 0, 0)),
                  pl.BlockSpec((1, d), const),
                  pl.BlockSpec((d, 3 * d), const),
                  pl.BlockSpec((CONV_WIDTH, d), const),
                  pl.BlockSpec((d, d), const),
                  pl.BlockSpec((1, d), const),
                  pl.BlockSpec((d, V7X_LANES), const),
                  pl.BlockSpec((1, V7X_LANES), const)],
        out_specs=(pl.BlockSpec((1, ts, d), lambda bi, si: (bi, si, 0)),
                   pl.BlockSpec((ts * nj, V7X_LANES), lambda bi, si: (bi * nst + si, 0)),
                   pl.BlockSpec((ts, V7X_LANES), lambda bi, si: (bi * nst + si, 0))),
        scratch_shapes=[pltpu.VMEM((8, d), F32), pltpu.VMEM((ts, d), BF16)],
        compiler_params=_params(2),
        name="conv_mixer",
    )(x, mod, g_mix, w_in, conv_w, w_out, g_ffn, wr, br)


def _route_kernel(lg_ref, tri_ref, plan_ref, cnt_ref, run_ref, pst_ref, *, blk):
    ph, i = pl.program_id(0), pl.program_id(1)
    tr, lanes = lg_ref.shape
    epg = EXPERTS_PER_GROUP

    @pl.when((ph == 0) & (i == 0))
    def _():
        run_ref[...] = jnp.zeros_like(run_ref)

    lt = lg_ref[...].T
    row8 = lax.broadcasted_iota(jnp.int32, (epg, tr), 0).astype(F32)
    neg = jnp.float32(-jnp.inf)
    big = jnp.float32(epg)
    gmask = row8 < N_GROUPS
    gl = jnp.where(gmask, lt[0:epg], neg)
    mg = jnp.max(gl, axis=0, keepdims=True)
    gidx = jnp.min(jnp.where(gl == mg, row8, big), axis=0, keepdims=True)
    den_g = jnp.sum(jnp.where(gmask, jnp.exp(lt[0:epg] - mg), 0.0), axis=0, keepdims=True)
    grp_p = 1.0 / den_g
    sel = lt[ROUTER_LANE_OFFSET:ROUTER_LANE_OFFSET + epg]
    for g in range(1, N_GROUPS):
        sel = jnp.where(gidx == g, lt[ROUTER_LANE_OFFSET + g * epg:ROUTER_LANE_OFFSET + (g + 1) * epg], sel)
    l1 = jnp.max(sel, axis=0, keepdims=True)
    i1 = jnp.min(jnp.where(sel == l1, row8, big), axis=0, keepdims=True)
    sel2 = jnp.where(row8 == i1, neg, sel)
    l2 = jnp.max(sel2, axis=0, keepdims=True)
    i2 = jnp.min(jnp.where(sel2 == l2, row8, big), axis=0, keepdims=True)
    r = jnp.exp(l2 - l1)
    w1 = grp_p / (1.0 + r)
    w2 = grp_p * r / (1.0 + r)
    e1 = gidx * epg + i1
    e2 = gidx * epg + i2
    row_e = lax.broadcasted_iota(jnp.int32, (N_EXPERTS, tr), 0).astype(F32)
    oh1 = (row_e == e1).astype(F32)
    oh2 = (row_e == e2).astype(F32)
    tot1 = jnp.sum(oh1, axis=1, keepdims=True)
    tot2 = jnp.sum(oh2, axis=1, keepdims=True)
    run = run_ref[:, 0:1]

    @pl.when(ph == 0)
    def _():
        new_run = run + tot1 + tot2
        run_ref[...] = jnp.broadcast_to(new_run, run_ref.shape)

        @pl.when(i == pl.num_programs(1) - 1)
        def _():
            cnt = jnp.broadcast_to(new_run, run_ref.shape)
            padded = jnp.floor((cnt + (blk - 1)) / blk) * blk
            rid = lax.broadcasted_iota(jnp.int32, cnt.shape, 0)
            csum = padded
            k = 1
            while k < N_EXPERTS:
                csum = csum + jnp.where(rid >= k, pltpu.roll(csum, k, 0), 0.0)
                k *= 2
            pst_ref[...] = csum - padded
            run_ref[...] = jnp.zeros_like(run_ref)
            lid = lax.broadcasted_iota(jnp.int32, cnt.shape, 1)
            cnt_ref[...] = jnp.where(lid == 0, cnt, jnp.where(lid == 1, csum, jnp.where(lid == 2, padded, 0.0)))

    @pl.when(ph == 1)
    def _():
        base = pst_ref[:, 0:1] + run
        oh = jnp.concatenate([oh1, oh2], axis=0).astype(BF16)
        off = jnp.concatenate([base, base + tot1], axis=0)
        pos1, pos2 = [], []
        for c in range(tr // lanes):
            ohc = oh[:, c * lanes:(c + 1) * lanes]
            before = jnp.dot(ohc, tri_ref[...], preferred_element_type=F32) + off
            ohf = ohc.astype(F32)
            pos1.append(jnp.sum(ohf[:N_EXPERTS] * before[:N_EXPERTS], axis=0, keepdims=True))
            pos2.append(jnp.sum(ohf[N_EXPERTS:] * before[N_EXPERTS:], axis=0, keepdims=True))
            off = off + jnp.sum(ohf, axis=1, keepdims=True)
        run_ref[...] = jnp.broadcast_to(run + tot1 + tot2, run_ref.shape)
        pos1 = jnp.concatenate(pos1, axis=1)
        pos2 = jnp.concatenate(pos2, axis=1)
        head = jnp.where(row8 == 0, pos1, jnp.where(row8 == 1, pos2,
                         jnp.where(row8 == 2, w1, jnp.where(row8 == 3, w2, 0.0))))
        plan_t = jnp.concatenate([head, jnp.zeros((lanes - epg, tr), F32)], axis=0)
        plan_ref[...] = plan_t.T


def _route(logits):
    t, lanes = logits.shape
    tr = min(ROUTE_TILE, t)
    nt = t // tr
    tri = (jnp.arange(lanes)[:, None] < jnp.arange(lanes)[None, :]).astype(BF16)
    return pl.pallas_call(
        functools.partial(_route_kernel, blk=MOE_BLOCK),
        out_shape=(jax.ShapeDtypeStruct((t, lanes), F32),
                   jax.ShapeDtypeStruct((N_EXPERTS, lanes), F32)),
        grid=(2, nt),
        in_specs=[pl.BlockSpec((tr, lanes), lambda ph, i: (i, 0)),
                  pl.BlockSpec((lanes, lanes), lambda ph, i: (0, 0))],
        out_specs=(pl.BlockSpec((tr, lanes), lambda ph, i: (i * ph, 0)),
                   pl.BlockSpec((N_EXPERTS, lanes), lambda ph, i: (0, 0))),
        scratch_shapes=[pltpu.VMEM((N_EXPERTS, lanes), F32), pltpu.VMEM((N_EXPERTS, lanes), F32)],
        compiler_params=_params(2),
        name="route",
    )(logits, tri)


def _dispatch_kernel(last_ref, row_ref, h2_ref, xs_hbm, zero_buf, sem, zero_sem, *, nj):
    ts = row_ref.shape[1]
    blk_rows = zero_buf.shape[0]

    @pl.when(pl.program_id(0) == 0)
    def _():
        z = jnp.zeros(zero_buf.shape, F32)
        zero_buf[...] = _pack_bf16_pair(z, z)

        def blank(e):
            dst = xs_hbm.at[pl.ds(pl.multiple_of(last_ref[e], nj), blk_rows)]
            return pltpu.make_async_copy(zero_buf, dst, zero_sem)

        def zstart(e, carry):
            @pl.when(last_ref[e] >= 0)
            def _():
                blank(e).start()
            return carry

        def zwait(e, carry):
            @pl.when(last_ref[e] >= 0)
            def _():
                blank(e).wait()
            return carry

        def tail(bi):
            dst = xs_hbm.at[pl.ds(pl.multiple_of(bi * blk_rows, blk_rows), blk_rows)]
            return pltpu.make_async_copy(zero_buf, dst, zero_sem)

        def tstart(bi, carry):
            tail(bi).start()
            return carry

        def twait(bi, carry):
            tail(bi).wait()
            return carry

        n_blocks = xs_hbm.shape[0] // blk_rows
        lax.fori_loop(0, N_EXPERTS, zstart, 0)
        lax.fori_loop(last_ref[N_EXPERTS], n_blocks, tstart, 0)
        lax.fori_loop(0, N_EXPERTS, zwait, 0)
        lax.fori_loop(last_ref[N_EXPERTS], n_blocks, twait, 0)

    chunk = min(DMA_ISSUE_CHUNK, ts)

    def start(c, carry):
        t0 = pl.multiple_of(c * chunk, chunk)
        for u in range(chunk):
            src = h2_ref.at[pl.ds(pl.multiple_of((t0 + u) * nj, nj), nj)]
            for k in range(2):
                pltpu.make_async_copy(src, xs_hbm.at[pl.ds(pl.multiple_of(row_ref[k, t0 + u], nj), nj)],
                                      sem).start(priority=k)
        return carry

    lax.fori_loop(0, ts // chunk, start, 0)
    for _ in range(2):
        pltpu.make_async_copy(h2_ref, xs_hbm.at[pl.ds(0, ts * nj)], sem).wait()


def _dispatch(last_row, row_t, h2, n_rows, nj):
    t = row_t.shape[1]
    ts = min(DMA_ROW_TILE, t)
    grid_spec = pltpu.PrefetchScalarGridSpec(
        num_scalar_prefetch=1,
        grid=(t // ts,),
        in_specs=[pl.BlockSpec((2, ts), lambda i, last: (0, i), memory_space=pltpu.SMEM),
                  pl.BlockSpec((ts * nj, V7X_LANES), lambda i, last: (i, 0))],
        out_specs=pl.BlockSpec(memory_space=pl.ANY),
        scratch_shapes=[pltpu.VMEM((MOE_BLOCK * nj, V7X_LANES), h2.dtype),
                        pltpu.SemaphoreType.DMA(()), pltpu.SemaphoreType.DMA(())])
    return pl.pallas_call(
        functools.partial(_dispatch_kernel, nj=nj),
        out_shape=jax.ShapeDtypeStruct((n_rows * nj, V7X_LANES), h2.dtype),
        grid_spec=grid_spec,
        compiler_params=_params(1),
        name="dispatch",
    )(last_row, row_t, h2)


def _experts_kernel(be_ref, nact_ref, xs_ref, wg_ref, wu_ref, wd_ref, ys_ref, wgb, wub, wdb, *, nj):
    i = pl.program_id(0)
    blk = xs_ref.shape[0] // nj

    @pl.when(i < nact_ref[0])
    def _():
        prev = be_ref[jnp.maximum(i - 1, 0)]

        @pl.when((i == 0) | (be_ref[i] != prev))
        def _():
            wgb[...] = wg_ref[0, 0].astype(BF16)
            wub[...] = wu_ref[0, 0].astype(BF16)
            wdb[...] = wd_ref[0, 0].astype(BF16)

        nchunk = EXPERT_ROW_CHUNKS if blk % (8 * EXPERT_ROW_CHUNKS) == 0 else 1
        rows = blk // nchunk
        xs = [jnp.concatenate([p.astype(BF16) for p in _load_row_groups(xs_ref, rows, nj, c * rows)], axis=1)
              for c in range(nchunk)]
        gs = [jnp.dot(x, wgb[...], preferred_element_type=F32) for x in xs]
        us = [jnp.dot(x, wub[...], preferred_element_type=F32) for x in xs]
        acts = [(g * jax.nn.sigmoid(g) * u).astype(BF16) for g, u in zip(gs, us)]
        ys = [jnp.dot(a, wdb[...], preferred_element_type=F32) for a in acts]
        for c, y in enumerate(ys):
            _store_row_groups(ys_ref, y, c * rows)

    @pl.when(i >= nact_ref[0])
    def _():
        z = jnp.zeros(ys_ref.shape, F32)
        ys_ref[...] = _pack_bf16_pair(z, z)


def _experts(block_e, n_active, xs, w_gate, w_up, w_down, layer):
    _, _, d, de = w_gate.shape
    nj = d // (2 * V7X_LANES)
    n_blocks = xs.shape[0] // (MOE_BLOCK * nj)
    grid_spec = pltpu.PrefetchScalarGridSpec(
        num_scalar_prefetch=2,
        grid=(n_blocks,),
        in_specs=[pl.BlockSpec((MOE_BLOCK * nj, V7X_LANES), lambda i, be, na: (i, 0)),
                  pl.BlockSpec((1, 1, d, de), lambda i, be, na: (layer, be[i], 0, 0)),
                  pl.BlockSpec((1, 1, d, de), lambda i, be, na: (layer, be[i], 0, 0)),
                  pl.BlockSpec((1, 1, de, d), lambda i, be, na: (layer, be[i], 0, 0))],
        out_specs=pl.BlockSpec((MOE_BLOCK * nj, V7X_LANES), lambda i, be, na: (i, 0)),
        scratch_shapes=[pltpu.VMEM((d, de), BF16), pltpu.VMEM((d, de), BF16),
                        pltpu.VMEM((de, d), BF16)])
    return pl.pallas_call(
        functools.partial(_experts_kernel, nj=nj),
        out_shape=jax.ShapeDtypeStruct(xs.shape, jnp.uint32),
        grid_spec=grid_spec,
        compiler_params=_params(1),
        name="experts",
    )(block_e, n_active, xs, w_gate, w_up, w_down)


def _start_row_gathers(row_ref, ys_hbm, buf, sem, ts, nj):
    chunk = min(DMA_ISSUE_CHUNK, ts)

    def start(c, carry):
        t0 = pl.multiple_of(c * chunk, chunk)
        for u in range(chunk):
            dst = pl.ds(pl.multiple_of((t0 + u) * nj, nj), nj)
            for k in range(2):
                pltpu.make_async_copy(ys_hbm.at[pl.ds(pl.multiple_of(row_ref[k, t0 + u], nj), nj)],
                                      buf.at[k, dst], sem).start(priority=k)
        return carry

    lax.fori_loop(0, ts // chunk, start, 0)


def _wait_row_gathers(ys_hbm, buf, sem, ts, nj):
    for k in range(2):
        pltpu.make_async_copy(ys_hbm.at[pl.ds(0, ts * nj)], buf.at[k], sem).wait()


def _weighted_expert_rows(buf, plan_ref, ts, nj):
    w1 = plan_ref[:, 2:3]
    w2 = plan_ref[:, 3:4]
    p1 = _load_row_groups(buf.at[0], ts, nj)
    p2 = _load_row_groups(buf.at[1], ts, nj)
    return jnp.concatenate([w1 * a + w2 * b for a, b in zip(p1, p2)], axis=1)


def _gathered_expert_rows(step, n_steps, rcur_ref, rnext_ref, plan_ref, ys_hbm, buf, sems, ts, nj):
    slot = lax.rem(step, 2)

    @pl.when(step == 0)
    def _():
        _start_row_gathers(rcur_ref, ys_hbm, buf.at[0], sems.at[0], ts, nj)

    @pl.when(step + 1 < n_steps)
    def _():
        _start_row_gathers(rnext_ref, ys_hbm, buf.at[1 - slot], sems.at[1 - slot], ts, nj)

    _wait_row_gathers(ys_hbm, buf.at[slot], sems.at[slot], ts, nj)
    return _weighted_expert_rows(buf.at[slot], plan_ref, ts, nj)


def _combine_kernel(rcur_ref, rnext_ref, x_ref, plan_ref, mod_ref, gfin_ref, ys_hbm, out_ref, buf, sems, *, nj):
    ts = x_ref.shape[1]
    moe = _gathered_expert_rows(pl.program_id(0), pl.num_programs(0), rcur_ref, rnext_ref, plan_ref,
                                ys_hbm, buf, sems, ts, nj)
    x2 = x_ref[0] + mod_ref[0][5:6] * moe
    ms = jnp.mean(x2 * x2, axis=-1, keepdims=True)
    out_ref[0] = x2 * lax.rsqrt(ms + NORM_EPS) * gfin_ref[...]


def _combine_final(row_t, x1, plan, mod, g_final, ys):
    b, s, d = x1.shape
    ts = min(DMA_ROW_TILE, s)
    nst = s // ts
    n_tiles = b * nst
    nj = d // (2 * V7X_LANES)
    return pl.pallas_call(
        functools.partial(_combine_kernel, nj=nj),
        out_shape=jax.ShapeDtypeStruct((b, s, d), F32),
        grid=(n_tiles,),
        in_specs=[pl.BlockSpec((2, ts), lambda i: (0, i), memory_space=pltpu.SMEM),
                  pl.BlockSpec((2, ts), lambda i: (0, jnp.minimum(i + 1, n_tiles - 1)),
                               memory_space=pltpu.SMEM),
                  pl.BlockSpec((1, ts, d), lambda i: (i // nst, i % nst, 0)),
                  pl.BlockSpec((ts, plan.shape[1]), lambda i: (i, 0)),
                  pl.BlockSpec((1, 6, d), lambda i: (i // nst, 0, 0)),
                  pl.BlockSpec((1, d), lambda i: (0, 0)),
                  pl.BlockSpec(memory_space=pl.ANY)],
        out_specs=pl.BlockSpec((1, ts, d), lambda i: (i // nst, i % nst, 0)),
        scratch_shapes=[pltpu.VMEM((2, 2, ts * nj, V7X_LANES), jnp.uint32),
                        pltpu.SemaphoreType.DMA((2,))],
        compiler_params=_params(1),
        name="combine_final",
    )(row_t, row_t, x1, plan, mod, g_final, ys)


def _moe_experts(h2, logits, w_gate, w_up, w_down, *, layer):
    t = logits.shape[0]
    nj = h2.shape[0] // t
    plan, cnt = _route(logits)
    n_blocks = -(-(t * 2) // MOE_BLOCK) + N_EXPERTS
    seg_end, seg_rows = cnt[:, 1], cnt[:, 2]
    pends = (seg_end * (1.0 / MOE_BLOCK)).astype(jnp.int32)
    block_e = jnp.minimum(jnp.sum(pends[None, :] <= jnp.arange(n_blocks)[:, None], axis=1),
                          N_EXPERTS - 1).astype(jnp.int32)
    n_active = pends[-1:]
    last_row = jnp.where(seg_rows > 0, (seg_end - MOE_BLOCK) * nj, -1.0).astype(jnp.int32)
    last_row = jnp.concatenate([last_row, n_active])
    row_t = (plan[:, 0:2] * nj).astype(jnp.int32).T
    xs = _dispatch(last_row, row_t, h2, n_blocks * MOE_BLOCK, nj)
    ys = _experts(block_e, n_active, xs, w_gate, w_up, w_down, layer)
    return row_t, plan, ys


def _qkv_kernel(rcur_ref, rnext_ref, x_ref, plan_ref, mod0_ref, mod_ref, g_ref, w_ref, *refs, nj):
    tabs, ys_hbm, x2_ref, (o0_ref, o1_ref, o2_ref) = refs[:9], refs[9], refs[10], refs[11:14]
    scr, buf, sems = refs[14:17]
    ts, d = x_ref.shape[1], x_ref.shape[2]
    step = pl.program_id(0) * pl.num_programs(1) + pl.program_id(1)
    n_steps = pl.num_programs(0) * pl.num_programs(1)
    moe = _gathered_expert_rows(step, n_steps, rcur_ref, rnext_ref, plan_ref, ys_hbm, buf, sems, ts, nj)
    x2 = x_ref[0] + mod0_ref[0][5:6] * moe
    x2_ref[0] = x2
    mod = mod_ref[0]
    hn = _norm_mod(x2, g_ref[...], mod[0:1], mod[1:2])
    wd = ATTN_OUT_DIM
    rep = wd // V7X_LANES
    nc = d // V7X_LANES
    for g, o_ref in enumerate((o0_ref, o1_ref, o2_ref)):
        dil = DILATED_GROUPS[g][1]
        n = ts // dil
        if dil == 1:
            h = hn.astype(BF16)
        else:
            if g == 1:
                for cc in range(nc):
                    scr[cc] = hn[:, cc * V7X_LANES:(cc + 1) * V7X_LANES]
            h = jnp.concatenate(
                [jnp.concatenate([scr[cc, pl.ds(r, n, stride=dil), :] for cc in range(nc)], axis=1)
                 for r in range(dil)], axis=0).astype(BF16)
        cos, sa, sb = (jnp.tile(t[...], (1, rep)) for t in tabs[3 * g:3 * g + 3])
        for j in range(3):
            slab = g * 3 + j
            acc = jnp.dot(h, w_ref[:, slab * wd:(slab + 1) * wd], preferred_element_type=F32)
            if j != 2:
                acc = (acc * cos + pltpu.roll(acc, ROT_DIM // 2, 1) * sa
                       + pltpu.roll(acc, wd - ROT_DIM // 2, 1) * sb)
            acc = acc.astype(BF16)
            for r in range(dil):
                o_ref[0, r, :, j * wd:(j + 1) * wd] = acc[r * n:(r + 1) * n]


def _rope_tables(s, ts):
    half = ROT_DIM // 2
    pos = jnp.arange(s, dtype=F32)
    inv_freq = jnp.power(jnp.float32(ROPE_THETA), -jnp.arange(0, ROT_DIM, 2, dtype=F32) / ROT_DIM)
    ang = pos[:, None] * inv_freq[None, :]
    cos, sin = jnp.cos(ang), jnp.sin(ang)
    m = jnp.arange(V7X_LANES) % HEAD_DIM
    cos_l = jnp.where(m[None, :] < ROT_DIM, cos[:, m % half], 1.0)
    sa_l = jnp.where((m[None, :] >= half) & (m[None, :] < ROT_DIM), sin[:, m % half], 0.0)
    sb_l = jnp.where(m[None, :] < half, -sin[:, m % half], 0.0)
    tabs = []
    for _, dil in DILATED_GROUPS:
        for t in (cos_l, sa_l, sb_l):
            t = t.astype(F32).reshape(s // ts, ts // dil, dil, V7X_LANES)
            tabs.append(jnp.transpose(t, (0, 2, 1, 3)).reshape(s, V7X_LANES))
    return tabs


def _combine_qkv_proj(row_t, x1, plan, ys, mod0, mod, g_mix, w_in):
    b, s, d = x1.shape
    ts = min(ROW_TILE, s)
    nst = s // ts
    nj = d // (2 * V7X_LANES)
    const = lambda si, bi: (0, 0)
    tab = pl.BlockSpec((ts, V7X_LANES), lambda si, bi: (si, 0))
    wd3 = 3 * ATTN_OUT_DIM

    def next_tile(si, bi):
        nb = lax.rem(bi + 1, b)
        ns = jnp.minimum(si + lax.div(bi + 1, b), nst - 1)
        return (0, nb * nst + ns)

    return pl.pallas_call(
        functools.partial(_qkv_kernel, nj=nj),
        out_shape=(jax.ShapeDtypeStruct((b, s, d), F32),)
                  + tuple(jax.ShapeDtypeStruct((b, dil, s // dil, wd3), BF16) for _, dil in DILATED_GROUPS),
        grid=(nst, b),
        in_specs=[pl.BlockSpec((2, ts), lambda si, bi: (0, bi * nst + si), memory_space=pltpu.SMEM),
                  pl.BlockSpec((2, ts), next_tile, memory_space=pltpu.SMEM),
                  pl.BlockSpec((1, ts, d), lambda si, bi: (bi, si, 0)),
                  pl.BlockSpec((ts, plan.shape[1]), lambda si, bi: (bi * nst + si, 0)),
                  pl.BlockSpec((1, 6, d), lambda si, bi: (bi, 0, 0)),
                  pl.BlockSpec((1, 6, d), lambda si, bi: (bi, 0, 0)),
                  pl.BlockSpec((1, d), const),
                  pl.BlockSpec((d, ATTN_IN_DIM), const)] + [tab] * 9
                 + [pl.BlockSpec(memory_space=pl.ANY)],
        out_specs=(pl.BlockSpec((1, ts, d), lambda si, bi: (bi, si, 0)),)
                  + tuple(pl.BlockSpec((1, dil, ts // dil, wd3), lambda si, bi: (bi, 0, si, 0))
                          for _, dil in DILATED_GROUPS),
        scratch_shapes=[pltpu.VMEM((d // V7X_LANES, ts, V7X_LANES), F32),
                        pltpu.VMEM((2, 2, ts * nj, V7X_LANES), jnp.uint32),
                        pltpu.SemaphoreType.DMA((2,))],
        compiler_params=_params(2),
        name="combine_qkv_proj",
    )(row_t, row_t, x1, plan, mod0, mod, g_mix, w_in, *_rope_tables(s, ts), ys)


def _attn_kernel(q_ref, kc_ref, kp_ref, vc_ref, vp_ref, o_ref, lse_ref):
    tq = q_ref.shape[2]
    sp = ATTN_SPAN
    nl = lse_ref.shape[3]
    first_tile = pl.program_id(2) == 0
    kext = jnp.concatenate([kp_ref[0, 0], kc_ref[0, 0]], axis=0)
    vext = jnp.concatenate([vp_ref[0, 0], vc_ref[0, 0]], axis=0)
    q_t = (q_ref[0, 0].astype(F32) * (HEAD_DIM ** -0.5)).T.astype(BF16)
    v_t = vext.astype(F32).T.astype(BF16)
    kj = lax.broadcasted_iota(jnp.int32, (2 * sp, sp), 0)
    qi = lax.broadcasted_iota(jnp.int32, (2 * sp, sp), 1)
    neg = jnp.float32(-jnp.inf)
    band = jnp.where(kj >= qi, jnp.where(kj <= qi + sp, 0.0, neg), neg)
    zeros_half = jnp.zeros((HEAD_DIM, sp), BF16)
    head_row = lax.broadcasted_iota(jnp.int32, (ATTN_HEADS, sp), 0)
    out_cols, lse_cols = [], []
    for i in range(tq // sp):
        if i == 0:
            bias = jnp.where(first_tile & (kj < sp), neg, band)
        else:
            bias = band
        scores = []
        for h in range(ATTN_HEADS):
            pair = h // 2
            k_pair = kext[i * sp:(i + 2) * sp, pair * 2 * HEAD_DIM:(pair + 1) * 2 * HEAD_DIM]
            q_h = q_t[h * HEAD_DIM:(h + 1) * HEAD_DIM, i * sp:(i + 1) * sp]
            q_m = jnp.concatenate([q_h, zeros_half] if h % 2 == 0 else [zeros_half, q_h], axis=0)
            scores.append(jnp.dot(k_pair, q_m, preferred_element_type=F32) + bias)
        probs, dens = [], []
        lse_t = jnp.zeros((ATTN_HEADS, sp), F32)
        for h, sc in enumerate(scores):
            m = jnp.max(sc, axis=0, keepdims=True)
            p = jnp.exp(sc - m)
            den = jnp.sum(p, axis=0, keepdims=True)
            probs.append(p.astype(BF16))
            dens.append(den)
            lse_t = jnp.where(head_row == h, m + jnp.log(den), lse_t)
        o_rows = []
        for h in range(ATTN_HEADS):
            v_h = v_t[h * HEAD_DIM:(h + 1) * HEAD_DIM, i * sp:(i + 2) * sp]
            o_rows.append(jnp.dot(v_h, probs[h], preferred_element_type=F32) / dens[h])
        out_cols.append(jnp.concatenate(o_rows, axis=0))
        lse_cols.append(jnp.concatenate([lse_t, jnp.zeros((nl - ATTN_HEADS, sp), F32)], axis=0))
    o_ref[0, 0] = jnp.concatenate(out_cols, axis=1).T.astype(BF16)
    lse_ref[0, 0] = jnp.concatenate(lse_cols, axis=1).T


def _dilated_attention(qkv_g):
    b, dil, L, _ = qkv_g.shape
    tq = min(ATTN_Q_TILE, L)
    sub = tq // ATTN_SPAN
    wd = ATTN_OUT_DIM

    def cur(j):
        return pl.BlockSpec((1, 1, tq, wd), lambda bi, r, lb: (bi, r, lb, j))

    def prev(j):
        return pl.BlockSpec((1, 1, ATTN_SPAN, wd),
                            lambda bi, r, lb: (bi, r, jnp.maximum(lb * sub - 1, 0), j))

    return pl.pallas_call(
        _attn_kernel,
        out_shape=(jax.ShapeDtypeStruct((b, dil, L, wd), BF16),
                   jax.ShapeDtypeStruct((b, dil, L, V7X_LANES), F32)),
        grid=(b, dil, L // tq),
        in_specs=[cur(0), cur(1), prev(1), cur(2), prev(2)],
        out_specs=(pl.BlockSpec((1, 1, tq, wd), lambda bi, r, lb: (bi, r, lb, 0)),
                   pl.BlockSpec((1, 1, tq, V7X_LANES), lambda bi, r, lb: (bi, r, lb, 0))),
        compiler_params=_params(3),
        name=f"dilated_attention_{dil}",
    )(qkv_g, qkv_g, qkv_g, qkv_g, qkv_g)


def _attn_out_kernel(o0_ref, o1_ref, o2_ref, l0_ref, l1_ref, l2_ref, x_ref, mod_ref, wout_ref, e_ref,
                     g2_ref, wr_ref, br_ref, x1_ref, h2_ref, lg_ref, o_scr, l_scr):
    ts = x_ref.shape[1]
    nc = ATTN_OUT_DIM // V7X_LANES
    for g, (o_ref, l_ref) in enumerate(((o0_ref, l0_ref), (o1_ref, l1_ref), (o2_ref, l2_ref))):
        dil = DILATED_GROUPS[g][1]
        for r in range(dil):
            rows = pl.ds(r, ts // dil, stride=dil) if dil > 1 else slice(None)
            o_r = o_ref[0, r].astype(F32)
            for cc in range(nc):
                o_scr[g, cc, rows, :] = o_r[:, cc * V7X_LANES:(cc + 1) * V7X_LANES]
            l_scr[g, rows, :] = l_ref[0, r]
    l0, l1, l2 = l_scr[0], l_scr[1], l_scr[2]
    m = jnp.maximum(jnp.maximum(l0, l1), l2)
    es = (jnp.exp(l0 - m), jnp.exp(l1 - m), jnp.exp(l2 - m))
    den = es[0] + es[1] + es[2]
    merged = jnp.zeros((ts, ATTN_OUT_DIM), F32)
    for g in range(N_DIL_GROUPS):
        w = es[g] / den
        w_hi = w.astype(BF16)
        w_lo = (w - w_hi.astype(F32)).astype(BF16)
        wfull = (jnp.dot(w_hi, e_ref[...], preferred_element_type=F32)
                 + jnp.dot(w_lo, e_ref[...], preferred_element_type=F32))
        merged = merged + wfull * jnp.concatenate([o_scr[g, cc] for cc in range(nc)], axis=1)
    y = jnp.dot(merged.astype(BF16), wout_ref[...], preferred_element_type=F32)
    mod = mod_ref[0]
    x1 = x_ref[0] + mod[2:3] * y
    x1_ref[0] = x1
    _ffn_input_and_logits(x1, mod, g2_ref, wr_ref, br_ref, h2_ref, lg_ref)


def _attn_out(outs, lses, x, mod, w_out, g_ffn, wr, br):
    b, s, d = x.shape
    ts = min(ROW_TILE, s)
    nst = s // ts
    nj = d // (2 * V7X_LANES)
    const = lambda bi, si: (0, 0)
    tok = lambda bi, si: (bi * nst + si, 0)
    stream = lambda bi, si: (bi, 0, si, 0)
    expand = (jnp.arange(V7X_LANES)[:, None] == jnp.arange(ATTN_OUT_DIM)[None, :] // HEAD_DIM).astype(BF16)
    return pl.pallas_call(
        _attn_out_kernel,
        out_shape=(jax.ShapeDtypeStruct((b, s, d), F32),
                   jax.ShapeDtypeStruct((b * s * nj, V7X_LANES), jnp.uint32),
                   jax.ShapeDtypeStruct((b * s, V7X_LANES), F32)),
        grid=(b, nst),
        in_specs=[pl.BlockSpec((1, dil, ts // dil, ATTN_OUT_DIM), stream) for _, dil in DILATED_GROUPS]
                 + [pl.BlockSpec((1, dil, ts // dil, V7X_LANES), stream) for _, dil in DILATED_GROUPS]
                 + [pl.BlockSpec((1, ts, d), lambda bi, si: (bi, si, 0)),
                    pl.BlockSpec((1, 6, d), lambda bi, si: (bi, 0, 0)),
                    pl.BlockSpec((ATTN_OUT_DIM, d), const),
                    pl.BlockSpec((V7X_LANES, ATTN_OUT_DIM), const),
                    pl.BlockSpec((1, d), const),
                    pl.BlockSpec((d, V7X_LANES), const),
                    pl.BlockSpec((1, V7X_LANES), const)],
        out_specs=(pl.BlockSpec((1, ts, d), lambda bi, si: (bi, si, 0)),
                   pl.BlockSpec((ts * nj, V7X_LANES), tok),
                   pl.BlockSpec((ts, V7X_LANES), tok)),
        scratch_shapes=[pltpu.VMEM((N_DIL_GROUPS, ATTN_OUT_DIM // V7X_LANES, ts, V7X_LANES), F32),
                        pltpu.VMEM((N_DIL_GROUPS, ts, V7X_LANES), F32)],
        compiler_params=_params(2),
        name="attn_out",
    )(*outs, *lses, x, mod, w_out, expand, g_ffn, wr, br)


def _router_params(w_grp, b_grp, w_exp, b_exp):
    d = w_grp.shape[0]
    w_e = jnp.transpose(w_exp, (1, 0, 2)).reshape(d, N_EXPERTS)
    pad1 = ROUTER_LANE_OFFSET - N_GROUPS
    pad2 = V7X_LANES - ROUTER_LANE_OFFSET - N_EXPERTS
    wr = jnp.concatenate([w_grp, jnp.zeros((d, pad1), F32), w_e, jnp.zeros((d, pad2), F32)], axis=1)
    br = jnp.concatenate([b_grp, jnp.zeros((pad1,), F32), b_exp.reshape(-1), jnp.zeros((pad2,), F32)])
    return wr.astype(BF16), br.reshape(1, V7X_LANES).astype(F32)


def kernel(x, c, norm_mix_g, norm_ffn_g, ada_w, ada_b, conv_in_w, conv_w, conv_out_w, attn_in_w,
           attn_out_w, router_grp_w, router_grp_b, router_exp_w, router_exp_b, exp_gate_w, exp_up_w,
           exp_down_w, final_norm_g):
    b, s, d = x.shape
    depth = ada_w.shape[0]
    assert depth == 2 and s % (DILATED_GROUPS[-1][1] * ATTN_SPAN) == 0 and d % (2 * V7X_LANES) == 0
    mod = _ada_modulation(c, ada_w, ada_b).reshape(depth, b, 6, d)
    g_fin = final_norm_g.reshape(1, d)

    wr, br = _router_params(router_grp_w[0], router_grp_b[0], router_exp_w[0], router_exp_b[0])
    x1, h2, logits = _conv_mixer(x, mod[0], norm_mix_g[0:1], conv_in_w[0].astype(BF16), conv_w[0],
                                 conv_out_w[0].astype(BF16), norm_ffn_g[0:1], wr, br)
    row_t, plan, ys = _moe_experts(h2, logits, exp_gate_w, exp_up_w, exp_down_w, layer=0)

    wr, br = _router_params(router_grp_w[1], router_grp_b[1], router_exp_w[1], router_exp_b[1])
    x2, *qkv = _combine_qkv_proj(row_t, x1, plan, ys, mod[0], mod[1], norm_mix_g[1:2],
                                 attn_in_w[0].astype(BF16))
    outs, lses = zip(*[_dilated_attention(qkv_g) for qkv_g in qkv])
    x3, h2, logits = _attn_out(outs, lses, x2, mod[1], attn_out_w[0].astype(BF16), norm_ffn_g[1:2],
                               wr, br)
    row_t, plan, ys = _moe_experts(h2, logits, exp_gate_w, exp_up_w, exp_down_w, layer=1)
    return _combine_final(row_t, x3, plan, mod[1], g_fin, ys)
```

```python
import functools

import jax
import jax.numpy as jnp
from jax import lax
from jax.experimental import pallas as pl
from jax.experimental.pallas import tpu as pltpu

CONV_WIDTH = 3
DILATED_GROUPS = ((128, 1), (512, 4), (2048, 16))
N_DIL_GROUPS = len(DILATED_GROUPS)
ATTN_HEADS = 8
HEAD_DIM = 64
ATTN_OUT_DIM = ATTN_HEADS * HEAD_DIM
ATTN_IN_DIM = N_DIL_GROUPS * 3 * ATTN_OUT_DIM
ROT_DIM = HEAD_DIM // 4
ROPE_THETA = 500000.0
N_GROUPS = 4
EXPERTS_PER_GROUP = 8
N_EXPERTS = N_GROUPS * EXPERTS_PER_GROUP
NORM_EPS = 1e-6
ATTN_SPAN = 128

V7X_LANES = 128
V7X_VMEM_LIMIT_BYTES = 56 * 1024 * 1024

ROW_TILE = 512
CONV_COL_BLOCK = 256
ROUTE_TILE = 1024
MOE_BLOCK = 512
EXPERT_ROW_CHUNKS = 2
ATTN_Q_TILE = 512
ROUTER_LANE_OFFSET = 8
DMA_ROW_TILE = 1024

F32 = jnp.float32
BF16 = jnp.bfloat16


def _params(n_axes):
    return pltpu.CompilerParams(dimension_semantics=("arbitrary",) * n_axes,
                                vmem_limit_bytes=V7X_VMEM_LIMIT_BYTES)


def _norm_mod(x, g, shift, scale):
    ms = jnp.mean(x * x, axis=-1, keepdims=True)
    y = x * lax.rsqrt(ms + NORM_EPS) * g
    return y * (1.0 + scale) + shift


def _pack_bf16_pair(a, b):
    return pltpu.pack_elementwise([a, b], packed_dtype=BF16)


def _unpack_bf16_pair(w):
    a = pltpu.unpack_elementwise(w, index=0, packed_dtype=BF16, unpacked_dtype=F32)
    b = pltpu.unpack_elementwise(w, index=1, packed_dtype=BF16, unpacked_dtype=F32)
    return a, b


def _store_row_groups(ref, x, first=0):
    rows, d = x.shape
    nj = d // (2 * V7X_LANES)
    for j in range(nj):
        lo = x[:, j * V7X_LANES:(j + 1) * V7X_LANES]
        hi = x[:, d // 2 + j * V7X_LANES:d // 2 + (j + 1) * V7X_LANES]
        ref[pl.ds(first * nj + j, rows, stride=nj), :] = _pack_bf16_pair(lo, hi)


def _load_row_groups(ref, rows, nj, first=0):
    los, his = [], []
    for j in range(nj):
        lo, hi = _unpack_bf16_pair(ref[pl.ds(first * nj + j, rows, stride=nj), :])
        los.append(lo)
        his.append(hi)
    return los + his


def _slot_rows_block(ts):
    return (2 * ts // V7X_LANES, V7X_LANES)


def _ffn_input_and_logits(x1, mod, g2_ref, wr_ref, br_ref, h2_ref, lg_ref):
    h2 = _norm_mod(x1, g2_ref[...], mod[3:4], mod[4:5])
    _store_row_groups(h2_ref, h2)
    lg_ref[...] = jnp.dot(h2.astype(BF16), wr_ref[...], preferred_element_type=F32) + br_ref[...]


def _ada_kernel(c_ref, w_ref, b_ref, o_ref):
    c = c_ref[...]
    ca = c * jax.nn.sigmoid(c)
    o_ref[0] = jnp.dot(ca, w_ref[0], preferred_element_type=F32,
                       precision=lax.Precision.HIGHEST) + b_ref[0]


def _ada_modulation(c, ada_w, ada_b):
    depth, d, n = ada_w.shape
    b = c.shape[0]
    tn = d
    return pl.pallas_call(
        _ada_kernel,
        out_shape=jax.ShapeDtypeStruct((depth, b, n), F32),
        grid=(depth, n // tn),
        in_specs=[pl.BlockSpec((b, d), lambda i, j: (0, 0)),
                  pl.BlockSpec((1, d, tn), lambda i, j: (i, 0, j)),
                  pl.BlockSpec((1, 1, tn), lambda i, j: (i, 0, j))],
        out_specs=pl.BlockSpec((1, b, tn), lambda i, j: (i, 0, j)),
        compiler_params=_params(2),
        name="ada_modulation",
    )(c, ada_w, ada_b.reshape(depth, 1, n))


def _conv_mixer_kernel(x_ref, mod_ref, g_ref, win_ref, cw_ref, wout_ref, g2_ref, wr_ref, br_ref,
                       x1_ref, h2_ref, lg_ref, carry_ref, v_ref, *, cb):
    ts, d = x_ref.shape[1], x_ref.shape[2]

    @pl.when(pl.program_id(1) == 0)
    def _():
        carry_ref[...] = jnp.zeros_like(carry_ref)

    x = x_ref[0]
    mod = mod_ref[0]
    h = _norm_mod(x, g_ref[...], mod[0:1], mod[1:2]).astype(BF16)
    row = lax.broadcasted_iota(jnp.int32, (ts, cb), 0)
    for j in range(d // cb):
        lo = j * cb
        bj = jnp.dot(h, win_ref[:, lo:lo + cb], preferred_element_type=F32)
        cj = jnp.dot(h, win_ref[:, d + lo:d + lo + cb], preferred_element_type=F32)
        uj = jnp.dot(h, win_ref[:, 2 * d + lo:2 * d + lo + cb], preferred_element_type=F32)
        z = cj * uj
        prev = carry_ref[:, lo:lo + cb]
        z1 = jnp.where(row == 0, prev[7:8], pltpu.roll(z, 1, 0))
        z2 = jnp.where(row == 0, prev[6:7], jnp.where(row == 1, prev[7:8], pltpu.roll(z, 2, 0)))
        zc = cw_ref[0:1, lo:lo + cb] * z2 + cw_ref[1:2, lo:lo + cb] * z1 + cw_ref[2:3, lo:lo + cb] * z
        carry_ref[:, lo:lo + cb] = z[ts - 8:ts]
        v_ref[:, lo:lo + cb] = (bj * zc).astype(BF16)
    y = jnp.dot(v_ref[...], wout_ref[...], preferred_element_type=F32)
    x1 = x + mod[2:3] * y
    x1_ref[0] = x1
    _ffn_input_and_logits(x1, mod, g2_ref, wr_ref, br_ref, h2_ref, lg_ref)


def _conv_mixer(x, mod, g_mix, w_in, conv_w, w_out, g_ffn, wr, br):
    b, s, d = x.shape
    ts = min(ROW_TILE, s)
    cb = min(CONV_COL_BLOCK, d)
    nst = s // ts
    nj = d // (2 * V7X_LANES)
    const = lambda bi, si: (0, 0)
    return pl.pallas_call(
        functools.partial(_conv_mixer_kernel, cb=cb),
        out_shape=(jax.ShapeDtypeStruct((b, s, d), F32),
                   jax.ShapeDtypeStruct((b * s * nj, V7X_LANES), jnp.uint32),
                   jax.ShapeDtypeStruct((b * s, V7X_LANES), F32)),
        grid=(b, nst),
        in_specs=[pl.BlockSpec((1, ts, d), lambda bi, si: (bi, si, 0)),
                  pl.BlockSpec((1, 6, d), lambda bi, si: (bi, 0, 0)),
                  pl.BlockSpec((1, d), const),
                  pl.BlockSpec((d, 3 * d), const),
                  pl.BlockSpec((CONV_WIDTH, d), const),
                  pl.BlockSpec((d, d), const),
                  pl.BlockSpec((1, d), const),
                  pl.BlockSpec((d, V7X_LANES), const),
                  pl.BlockSpec((1, V7X_LANES), const)],
        out_specs=(pl.BlockSpec((1, ts, d), lambda bi, si: (bi, si, 0)),
                   pl.BlockSpec((ts * nj, V7X_LANES), lambda bi, si: (bi * nst + si, 0)),
                   pl.BlockSpec((ts, V7X_LANES), lambda bi, si: (bi * nst + si, 0))),
        scratch_shapes=[pltpu.VMEM((8, d), F32), pltpu.VMEM((ts, d), BF16)],
        compiler_params=_params(2),
        name="conv_mixer",
    )(x, mod, g_mix, w_in, conv_w, w_out, g_ffn, wr, br)


def _route_kernel(lg_ref, tri_ref, plan_ref, cnt_ref, run_ref, pst_ref, *, blk):
    ph, i = pl.program_id(0), pl.program_id(1)
    tr, lanes = lg_ref.shape
    epg = EXPERTS_PER_GROUP

    @pl.when((ph == 0) & (i == 0))
    def _():
        run_ref[...] = jnp.zeros_like(run_ref)

    lt = lg_ref[...].T
    row8 = lax.broadcasted_iota(jnp.int32, (epg, tr), 0).astype(F32)
    neg = jnp.float32(-jnp.inf)
    big = jnp.float32(epg)
    gmask = row8 < N_GROUPS
    gl = jnp.where(gmask, lt[0:epg], neg)
    mg = jnp.max(gl, axis=0, keepdims=True)
    gidx = jnp.min(jnp.where(gl == mg, row8, big), axis=0, keepdims=True)
    den_g = jnp.sum(jnp.where(gmask, jnp.exp(lt[0:epg] - mg), 0.0), axis=0, keepdims=True)
    grp_p = 1.0 / den_g
    sel = lt[ROUTER_LANE_OFFSET:ROUTER_LANE_OFFSET + epg]
    for g in range(1, N_GROUPS):
        sel = jnp.where(gidx == g, lt[ROUTER_LANE_OFFSET + g * epg:ROUTER_LANE_OFFSET + (g + 1) * epg], sel)
    l1 = jnp.max(sel, axis=0, keepdims=True)
    i1 = jnp.min(jnp.where(sel == l1, row8, big), axis=0, keepdims=True)
    sel2 = jnp.where(row8 == i1, neg, sel)
    l2 = jnp.max(sel2, axis=0, keepdims=True)
    i2 = jnp.min(jnp.where(sel2 == l2, row8, big), axis=0, keepdims=True)
    r = jnp.exp(l2 - l1)
    w1 = grp_p / (1.0 + r)
    w2 = grp_p * r / (1.0 + r)
    e1 = gidx * epg + i1
    e2 = gidx * epg + i2
    row_e = lax.broadcasted_iota(jnp.int32, (N_EXPERTS, tr), 0).astype(F32)
    oh1 = (row_e == e1).astype(F32)
    oh2 = (row_e == e2).astype(F32)
    tot1 = jnp.sum(oh1, axis=1, keepdims=True)
    tot2 = jnp.sum(oh2, axis=1, keepdims=True)
    run = run_ref[:, 0:1]

    @pl.when(ph == 0)
    def _():
        new_run = run + tot1 + tot2
        run_ref[...] = jnp.broadcast_to(new_run, run_ref.shape)

        @pl.when(i == pl.num_programs(1) - 1)
        def _():
            cnt = jnp.broadcast_to(new_run, run_ref.shape)
            padded = jnp.floor((cnt + (blk - 1)) / blk) * blk
            rid = lax.broadcasted_iota(jnp.int32, cnt.shape, 0)
            csum = padded
            k = 1
            while k < N_EXPERTS:
                csum = csum + jnp.where(rid >= k, pltpu.roll(csum, k, 0), 0.0)
                k *= 2
            pst_ref[...] = csum - padded
            run_ref[...] = jnp.zeros_like(run_ref)
            lid = lax.broadcasted_iota(jnp.int32, cnt.shape, 1)
            cnt_ref[...] = jnp.where(lid == 0, cnt, jnp.where(lid == 1, csum, jnp.where(lid == 2, padded, 0.0)))

    @pl.when(ph == 1)
    def _():
        base = pst_ref[:, 0:1] + run
        oh = jnp.concatenate([oh1, oh2], axis=0).astype(BF16)
        off = jnp.concatenate([base, base + tot1], axis=0)
        pos1, pos2 = [], []
        for c in range(tr // lanes):
            ohc = oh[:, c * lanes:(c + 1) * lanes]
            before = jnp.dot(ohc, tri_ref[...], preferred_element_type=F32) + off
            ohf = ohc.astype(F32)
            pos1.append(jnp.sum(ohf[:N_EXPERTS] * before[:N_EXPERTS], axis=0, keepdims=True))
            pos2.append(jnp.sum(ohf[N_EXPERTS:] * before[N_EXPERTS:], axis=0, keepdims=True))
            off = off + jnp.sum(ohf, axis=1, keepdims=True)
        run_ref[...] = jnp.broadcast_to(run + tot1 + tot2, run_ref.shape)
        pos1 = jnp.concatenate(pos1, axis=1)
        pos2 = jnp.concatenate(pos2, axis=1)
        head = jnp.where(row8 == 0, pos1, jnp.where(row8 == 1, pos2,
                         jnp.where(row8 == 2, w1, jnp.where(row8 == 3, w2, 0.0))))
        plan_t = jnp.concatenate([head, jnp.zeros((lanes - epg, tr), F32)], axis=0)
        plan_ref[...] = plan_t.T


def _route(logits):
    t, lanes = logits.shape
    tr = min(ROUTE_TILE, t)
    nt = t // tr
    tri = (jnp.arange(lanes)[:, None] < jnp.arange(lanes)[None, :]).astype(BF16)
    return pl.pallas_call(
        functools.partial(_route_kernel, blk=MOE_BLOCK),
        out_shape=(jax.ShapeDtypeStruct((t, lanes), F32),
                   jax.ShapeDtypeStruct((N_EXPERTS, lanes), F32)),
        grid=(2, nt),
        in_specs=[pl.BlockSpec((tr, lanes), lambda ph, i: (i, 0)),
                  pl.BlockSpec((lanes, lanes), lambda ph, i: (0, 0))],
        out_specs=(pl.BlockSpec((tr, lanes), lambda ph, i: (i * ph, 0)),
                   pl.BlockSpec((N_EXPERTS, lanes), lambda ph, i: (0, 0))),
        scratch_shapes=[pltpu.VMEM((N_EXPERTS, lanes), F32), pltpu.VMEM((N_EXPERTS, lanes), F32)],
        compiler_params=_params(2),
        name="route",
    )(logits, tri)


def _dispatch_kernel(last_ref, row_ref, h2_ref, xs_hbm, zero_buf, sem, zero_sem, *, nj):
    ts = h2_ref.shape[0] // nj
    blk_rows = zero_buf.shape[0]

    @pl.when(pl.program_id(0) == 0)
    def _():
        z = jnp.zeros(zero_buf.shape, F32)
        zero_buf[...] = _pack_bf16_pair(z, z)

        def blank(e):
            dst = xs_hbm.at[pl.ds(pl.multiple_of(last_ref[e], nj), blk_rows)]
            return pltpu.make_async_copy(zero_buf, dst, zero_sem)

        def zstart(e, carry):
            @pl.when(last_ref[e] >= 0)
            def _():
                blank(e).start()
            return carry

        def zwait(e, carry):
            @pl.when(last_ref[e] >= 0)
            def _():
                blank(e).wait()
            return carry

        def tail(bi):
            dst = xs_hbm.at[pl.ds(pl.multiple_of(bi * blk_rows, blk_rows), blk_rows)]
            return pltpu.make_async_copy(zero_buf, dst, zero_sem)

        def tstart(bi, carry):
            tail(bi).start()
            return carry

        def twait(bi, carry):
            tail(bi).wait()
            return carry

        n_blocks = xs_hbm.shape[0] // blk_rows
        lax.fori_loop(0, N_EXPERTS, zstart, 0)
        lax.fori_loop(last_ref[N_EXPERTS], n_blocks, tstart, 0)
        lax.fori_loop(0, N_EXPERTS, zwait, 0)
        lax.fori_loop(last_ref[N_EXPERTS], n_blocks, twait, 0)

    chunk = row_ref.shape[1]

    def start(c, carry):
        t0 = pl.multiple_of(c * chunk, chunk)
        for u in range(chunk):
            src = h2_ref.at[pl.ds(pl.multiple_of((t0 + u) * nj, nj), nj)]
            for k in range(2):
                pltpu.make_async_copy(src, xs_hbm.at[pl.ds(pl.multiple_of(row_ref[2 * c + k, u], nj), nj)],
                                      sem).start(priority=k)
        return carry

    lax.fori_loop(0, ts // chunk, start, 0)
    for _ in range(2):
        pltpu.make_async_copy(h2_ref, xs_hbm.at[pl.ds(0, ts * nj)], sem).wait()


def _dispatch(last_row, row_t, h2, n_rows, nj):
    t = h2.shape[0] // nj
    ts = min(DMA_ROW_TILE, t)
    grid_spec = pltpu.PrefetchScalarGridSpec(
        num_scalar_prefetch=1,
        grid=(t // ts,),
        in_specs=[pl.BlockSpec(_slot_rows_block(ts), lambda i, last: (i, 0), memory_space=pltpu.SMEM),
                  pl.BlockSpec((ts * nj, V7X_LANES), lambda i, last: (i, 0))],
        out_specs=pl.BlockSpec(memory_space=pl.ANY),
        scratch_shapes=[pltpu.VMEM((MOE_BLOCK * nj, V7X_LANES), h2.dtype),
                        pltpu.SemaphoreType.DMA(()), pltpu.SemaphoreType.DMA(())])
    return pl.pallas_call(
        functools.partial(_dispatch_kernel, nj=nj),
        out_shape=jax.ShapeDtypeStruct((n_rows * nj, V7X_LANES), h2.dtype),
        grid_spec=grid_spec,
        compiler_params=_params(1),
        name="dispatch",
    )(last_row, row_t, h2)


def _experts_kernel(be_ref, nact_ref, xs_ref, wg_ref, wu_ref, wd_ref, ys_ref, wgb, wub, wdb, *, nj):
    i = pl.program_id(0)
    blk = xs_ref.shape[0] // nj

    @pl.when(i < nact_ref[0])
    def _():
        prev = be_ref[jnp.maximum(i - 1, 0)]

        @pl.when((i == 0) | (be_ref[i] != prev))
        def _():
            wgb[...] = wg_ref[0, 0].astype(BF16)
            wub[...] = wu_ref[0, 0].astype(BF16)
            wdb[...] = wd_ref[0, 0].astype(BF16)

        nchunk = EXPERT_ROW_CHUNKS if blk % (8 * EXPERT_ROW_CHUNKS) == 0 else 1
        rows = blk // nchunk
        xs = [jnp.concatenate([p.astype(BF16) for p in _load_row_groups(xs_ref, rows, nj, c * rows)], axis=1)
              for c in range(nchunk)]
        gs = [jnp.dot(x, wgb[...], preferred_element_type=F32) for x in xs]
        us = [jnp.dot(x, wub[...], preferred_element_type=F32) for x in xs]
        acts = [(g * jax.nn.sigmoid(g) * u).astype(BF16) for g, u in zip(gs, us)]
        ys = [jnp.dot(a, wdb[...], preferred_element_type=F32) for a in acts]
        for c, y in enumerate(ys):
            _store_row_groups(ys_ref, y, c * rows)

    @pl.when(i >= nact_ref[0])
    def _():
        z = jnp.zeros(ys_ref.shape, F32)
        ys_ref[...] = _pack_bf16_pair(z, z)


def _experts(block_e, n_active, xs, w_gate, w_up, w_down, layer):
    _, _, d, de = w_gate.shape
    nj = d // (2 * V7X_LANES)
    n_blocks = xs.shape[0] // (MOE_BLOCK * nj)
    grid_spec = pltpu.PrefetchScalarGridSpec(
        num_scalar_prefetch=2,
        grid=(n_blocks,),
        in_specs=[pl.BlockSpec((MOE_BLOCK * nj, V7X_LANES), lambda i, be, na: (i, 0)),
                  pl.BlockSpec((1, 1, d, de), lambda i, be, na: (layer, be[i], 0, 0)),
                  pl.BlockSpec((1, 1, d, de), lambda i, be, na: (layer, be[i], 0, 0)),
                  pl.BlockSpec((1, 1, de, d), lambda i, be, na: (layer, be[i], 0, 0))],
        out_specs=pl.BlockSpec((MOE_BLOCK * nj, V7X_LANES), lambda i, be, na: (i, 0)),
        scratch_shapes=[pltpu.VMEM((d, de), BF16), pltpu.VMEM((d, de), BF16),
                        pltpu.VMEM((de, d), BF16)])
    return pl.pallas_call(
        functools.partial(_experts_kernel, nj=nj),
        out_shape=jax.ShapeDtypeStruct(xs.shape, jnp.uint32),
        grid_spec=grid_spec,
        compiler_params=_params(1),
        name="experts",
    )(block_e, n_active, xs, w_gate, w_up, w_down)


def _start_row_gathers(row_ref, ys_hbm, buf, sem, ts, nj):
    chunk = row_ref.shape[1]

    def start(c, carry):
        t0 = pl.multiple_of(c * chunk, chunk)
        for u in range(chunk):
            dst = pl.ds(pl.multiple_of((t0 + u) * nj, nj), nj)
            for k in range(2):
                pltpu.make_async_copy(ys_hbm.at[pl.ds(pl.multiple_of(row_ref[2 * c + k, u], nj), nj)],
                                      buf.at[k, dst], sem).start(priority=k)
        return carry

    lax.fori_loop(0, ts // chunk, start, 0)


def _wait_row_gathers(ys_hbm, buf, sem, ts, nj):
    for k in range(2):
        pltpu.make_async_copy(ys_hbm.at[pl.ds(0, ts * nj)], buf.at[k], sem).wait()


def _weighted_expert_rows(buf, plan_ref, ts, nj):
    w1 = plan_ref[:, 2:3]
    w2 = plan_ref[:, 3:4]
    p1 = _load_row_groups(buf.at[0], ts, nj)
    p2 = _load_row_groups(buf.at[1], ts, nj)
    return jnp.concatenate([w1 * a + w2 * b for a, b in zip(p1, p2)], axis=1)


def _gathered_expert_rows(step, n_steps, rcur_ref, rnext_ref, plan_ref, ys_hbm, buf, sems, ts, nj):
    slot = lax.rem(step, 2)

    @pl.when(step == 0)
    def _():
        _start_row_gathers(rcur_ref, ys_hbm, buf.at[0], sems.at[0], ts, nj)

    @pl.when(step + 1 < n_steps)
    def _():
        _start_row_gathers(rnext_ref, ys_hbm, buf.at[1 - slot], sems.at[1 - slot], ts, nj)

    _wait_row_gathers(ys_hbm, buf.at[slot], sems.at[slot], ts, nj)
    return _weighted_expert_rows(buf.at[slot], plan_ref, ts, nj)


def _combine_kernel(rcur_ref, rnext_ref, x_ref, plan_ref, mod_ref, gfin_ref, ys_hbm, out_ref, buf, sems, *, nj):
    ts = x_ref.shape[1]
    moe = _gathered_expert_rows(pl.program_id(0), pl.num_programs(0), rcur_ref, rnext_ref, plan_ref,
                                ys_hbm, buf, sems, ts, nj)
    x2 = x_ref[0] + mod_ref[0][5:6] * moe
    ms = jnp.mean(x2 * x2, axis=-1, keepdims=True)
    out_ref[0] = x2 * lax.rsqrt(ms + NORM_EPS) * gfin_ref[...]


def _combine_final(row_t, x1, plan, mod, g_final, ys):
    b, s, d = x1.shape
    ts = min(DMA_ROW_TILE, s)
    nst = s // ts
    n_tiles = b * nst
    nj = d // (2 * V7X_LANES)
    return pl.pallas_call(
        functools.partial(_combine_kernel, nj=nj),
        out_shape=jax.ShapeDtypeStruct((b, s, d), F32),
        grid=(n_tiles,),
        in_specs=[pl.BlockSpec(_slot_rows_block(ts), lambda i: (i, 0), memory_space=pltpu.SMEM),
                  pl.BlockSpec(_slot_rows_block(ts), lambda i: (jnp.minimum(i + 1, n_tiles - 1), 0),
                               memory_space=pltpu.SMEM),
                  pl.BlockSpec((1, ts, d), lambda i: (i // nst, i % nst, 0)),
                  pl.BlockSpec((ts, plan.shape[1]), lambda i: (i, 0)),
                  pl.BlockSpec((1, 6, d), lambda i: (i // nst, 0, 0)),
                  pl.BlockSpec((1, d), lambda i: (0, 0)),
                  pl.BlockSpec(memory_space=pl.ANY)],
        out_specs=pl.BlockSpec((1, ts, d), lambda i: (i // nst, i % nst, 0)),
        scratch_shapes=[pltpu.VMEM((2, 2, ts * nj, V7X_LANES), jnp.uint32),
                        pltpu.SemaphoreType.DMA((2,))],
        compiler_params=_params(1),
        name="combine_final",
    )(row_t, row_t, x1, plan, mod, g_final, ys)


def _moe_experts(h2, logits, w_gate, w_up, w_down, *, layer):
    t = logits.shape[0]
    nj = h2.shape[0] // t
    plan, cnt = _route(logits)
    n_blocks = -(-(t * 2) // MOE_BLOCK) + N_EXPERTS
    seg_end, seg_rows = cnt[:, 1], cnt[:, 2]
    pends = (seg_end * (1.0 / MOE_BLOCK)).astype(jnp.int32)
    block_e = jnp.minimum(jnp.sum(pends[None, :] <= jnp.arange(n_blocks)[:, None], axis=1),
                          N_EXPERTS - 1).astype(jnp.int32)
    n_active = pends[-1:]
    last_row = jnp.where(seg_rows > 0, (seg_end - MOE_BLOCK) * nj, -1.0).astype(jnp.int32)
    last_row = jnp.concatenate([last_row, n_active])
    row_t = (plan[:, 0:2] * nj).astype(jnp.int32).reshape(t // V7X_LANES, V7X_LANES, 2)
    row_t = jnp.transpose(row_t, (0, 2, 1)).reshape(2 * t // V7X_LANES, V7X_LANES)
    xs = _dispatch(last_row, row_t, h2, n_blocks * MOE_BLOCK, nj)
    ys = _experts(block_e, n_active, xs, w_gate, w_up, w_down, layer)
    return row_t, plan, ys


def _qkv_kernel(rcur_ref, rnext_ref, x_ref, plan_ref, mod0_ref, mod_ref, g_ref, w_ref, *refs, nj):
    tabs, ys_hbm, x2_ref, (o0_ref, o1_ref, o2_ref) = refs[:9], refs[9], refs[10], refs[11:14]
    scr, buf, sems = refs[14:17]
    ts, d = x_ref.shape[1], x_ref.shape[2]
    step = pl.program_id(0) * pl.num_programs(1) + pl.program_id(1)
    n_steps = pl.num_programs(0) * pl.num_programs(1)
    moe = _gathered_expert_rows(step, n_steps, rcur_ref, rnext_ref, plan_ref, ys_hbm, buf, sems, ts, nj)
    x2 = x_ref[0] + mod0_ref[0][5:6] * moe
    x2_ref[0] = x2
    mod = mod_ref[0]
    hn = _norm_mod(x2, g_ref[...], mod[0:1], mod[1:2])
    wd = ATTN_OUT_DIM
    rep = wd // V7X_LANES
    nc = d // V7X_LANES
    for g, o_ref in enumerate((o0_ref, o1_ref, o2_ref)):
        dil = DILATED_GROUPS[g][1]
        n = ts // dil
        if dil == 1:
            h = hn.astype(BF16)
        else:
            if g == 1:
                for cc in range(nc):
                    scr[cc] = hn[:, cc * V7X_LANES:(cc + 1) * V7X_LANES]
            h = jnp.concatenate(
                [jnp.concatenate([scr[cc, pl.ds(r, n, stride=dil), :] for cc in range(nc)], axis=1)
                 for r in range(dil)], axis=0).astype(BF16)
        cos, sa, sb = (jnp.tile(t[...], (1, rep)) for t in tabs[3 * g:3 * g + 3])
        for j in range(3):
            slab = g * 3 + j
            acc = jnp.dot(h, w_ref[:, slab * wd:(slab + 1) * wd], preferred_element_type=F32)
            if j != 2:
                acc = (acc * cos + pltpu.roll(acc, ROT_DIM // 2, 1) * sa
                       + pltpu.roll(acc, wd - ROT_DIM // 2, 1) * sb)
            acc = acc.astype(BF16)
            for r in range(dil):
                o_ref[0, r, :, j * wd:(j + 1) * wd] = acc[r * n:(r + 1) * n]


def _rope_tables(s, ts):
    half = ROT_DIM // 2
    pos = jnp.arange(s, dtype=F32)
    inv_freq = jnp.power(jnp.float32(ROPE_THETA), -jnp.arange(0, ROT_DIM, 2, dtype=F32) / ROT_DIM)
    ang = pos[:, None] * inv_freq[None, :]
    cos, sin = jnp.cos(ang), jnp.sin(ang)
    m = jnp.arange(V7X_LANES) % HEAD_DIM
    cos_l = jnp.where(m[None, :] < ROT_DIM, cos[:, m % half], 1.0)
    sa_l = jnp.where((m[None, :] >= half) & (m[None, :] < ROT_DIM), sin[:, m % half], 0.0)
    sb_l = jnp.where(m[None, :] < half, -sin[:, m % half], 0.0)
    tabs = []
    for _, dil in DILATED_GROUPS:
        for t in (cos_l, sa_l, sb_l):
            t = t.astype(F32).reshape(s // ts, ts // dil, dil, V7X_LANES)
            tabs.append(jnp.transpose(t, (0, 2, 1, 3)).reshape(s, V7X_LANES))
    return tabs


def _combine_qkv_proj(row_t, x1, plan, ys, mod0, mod, g_mix, w_in):
    b, s, d = x1.shape
    ts = min(ROW_TILE, s)
    nst = s // ts
    nj = d // (2 * V7X_LANES)
    const = lambda si, bi: (0, 0)
    tab = pl.BlockSpec((ts, V7X_LANES), lambda si, bi: (si, 0))
    wd3 = 3 * ATTN_OUT_DIM

    def next_tile(si, bi):
        nb = lax.rem(bi + 1, b)
        ns = jnp.minimum(si + lax.div(bi + 1, b), nst - 1)
        return (nb * nst + ns, 0)

    return pl.pallas_call(
        functools.partial(_qkv_kernel, nj=nj),
        out_shape=(jax.ShapeDtypeStruct((b, s, d), F32),)
                  + tuple(jax.ShapeDtypeStruct((b, dil, s // dil, wd3), BF16) for _, dil in DILATED_GROUPS),
        grid=(nst, b),
        in_specs=[pl.BlockSpec(_slot_rows_block(ts), lambda si, bi: (bi * nst + si, 0),
                               memory_space=pltpu.SMEM),
                  pl.BlockSpec(_slot_rows_block(ts), next_tile, memory_space=pltpu.SMEM),
                  pl.BlockSpec((1, ts, d), lambda si, bi: (bi, si, 0)),
                  pl.BlockSpec((ts, plan.shape[1]), lambda si, bi: (bi * nst + si, 0)),
                  pl.BlockSpec((1, 6, d), lambda si, bi: (bi, 0, 0)),
                  pl.BlockSpec((1, 6, d), lambda si, bi: (bi, 0, 0)),
                  pl.BlockSpec((1, d), const),
                  pl.BlockSpec((d, ATTN_IN_DIM), const)] + [tab] * 9
                 + [pl.BlockSpec(memory_space=pl.ANY)],
        out_specs=(pl.BlockSpec((1, ts, d), lambda si, bi: (bi, si, 0)),)
                  + tuple(pl.BlockSpec((1, dil, ts // dil, wd3), lambda si, bi: (bi, 0, si, 0))
                          for _, dil in DILATED_GROUPS),
        scratch_shapes=[pltpu.VMEM((d // V7X_LANES, ts, V7X_LANES), F32),
                        pltpu.VMEM((2, 2, ts * nj, V7X_LANES), jnp.uint32),
                        pltpu.SemaphoreType.DMA((2,))],
        compiler_params=_params(2),
        name="combine_qkv_proj",
    )(row_t, row_t, x1, plan, mod0, mod, g_mix, w_in, *_rope_tables(s, ts), ys)


def _attn_kernel(q_ref, kc_ref, kp_ref, vc_ref, vp_ref, o_ref, lse_ref):
    tq = q_ref.shape[2]
    sp = ATTN_SPAN
    nl = lse_ref.shape[3]
    first_tile = pl.program_id(2) == 0
    kext = jnp.concatenate([kp_ref[0, 0], kc_ref[0, 0]], axis=0)
    vext = jnp.concatenate([vp_ref[0, 0], vc_ref[0, 0]], axis=0)
    q_t = (q_ref[0, 0].astype(F32) * (HEAD_DIM ** -0.5)).T.astype(BF16)
    v_t = vext.astype(F32).T.astype(BF16)
    kj = lax.broadcasted_iota(jnp.int32, (2 * sp, sp), 0)
    qi = lax.broadcasted_iota(jnp.int32, (2 * sp, sp), 1)
    neg = jnp.float32(-jnp.inf)
    band = jnp.where(kj >= qi, jnp.where(kj <= qi + sp, 0.0, neg), neg)
    zeros_half = jnp.zeros((HEAD_DIM, sp), BF16)
    head_row = lax.broadcasted_iota(jnp.int32, (ATTN_HEADS, sp), 0)
    out_cols, lse_cols = [], []
    for i in range(tq // sp):
        if i == 0:
            bias = jnp.where(first_tile & (kj < sp), neg, band)
        else:
            bias = band
        scores = []
        for h in range(ATTN_HEADS):
            pair = h // 2
            k_pair = kext[i * sp:(i + 2) * sp, pair * 2 * HEAD_DIM:(pair + 1) * 2 * HEAD_DIM]
            q_h = q_t[h * HEAD_DIM:(h + 1) * HEAD_DIM, i * sp:(i + 1) * sp]
            q_m = jnp.concatenate([q_h, zeros_half] if h % 2 == 0 else [zeros_half, q_h], axis=0)
            scores.append(jnp.dot(k_pair, q_m, preferred_element_type=F32) + bias)
        probs, dens = [], []
        lse_t = jnp.zeros((ATTN_HEADS, sp), F32)
        for h, sc in enumerate(scores):
            m = jnp.max(sc, axis=0, keepdims=True)
            p = jnp.exp(sc - m)
            den = jnp.sum(p, axis=0, keepdims=True)
            probs.append(p.astype(BF16))
            dens.append(den)
            lse_t = jnp.where(head_row == h, m + jnp.log(den), lse_t)
        o_rows = []
        for h in range(ATTN_HEADS):
            v_h = v_t[h * HEAD_DIM:(h + 1) * HEAD_DIM, i * sp:(i + 2) * sp]
            o_rows.append(jnp.dot(v_h, probs[h], preferred_element_type=F32) / dens[h])
        out_cols.append(jnp.concatenate(o_rows, axis=0))
        lse_cols.append(jnp.concatenate([lse_t, jnp.zeros((nl - ATTN_HEADS, sp), F32)], axis=0))
    o_ref[0, 0] = jnp.concatenate(out_cols, axis=1).T.astype(BF16)
    lse_ref[0, 0] = jnp.concatenate(lse_cols, axis=1).T


def _dilated_attention(qkv_g):
    b, dil, L, _ = qkv_g.shape
    tq = min(ATTN_Q_TILE, L)
    sub = tq // ATTN_SPAN
    wd = ATTN_OUT_DIM

    def cur(j):
        return pl.BlockSpec((1, 1, tq, wd), lambda bi, r, lb: (bi, r, lb, j))

    def prev(j):
        return pl.BlockSpec((1, 1, ATTN_SPAN, wd),
                            lambda bi, r, lb: (bi, r, jnp.maximum(lb * sub - 1, 0), j))

    return pl.pallas_call(
        _attn_kernel,
        out_shape=(jax.ShapeDtypeStruct((b, dil, L, wd), BF16),
                   jax.ShapeDtypeStruct((b, dil, L, V7X_LANES), F32)),
        grid=(b, dil, L // tq),
        in_specs=[cur(0), cur(1), prev(1), cur(2), prev(2)],
        out_specs=(pl.BlockSpec((1, 1, tq, wd), lambda bi, r, lb: (bi, r, lb, 0)),
                   pl.BlockSpec((1, 1, tq, V7X_LANES), lambda bi, r, lb: (bi, r, lb, 0))),
        compiler_params=_params(3),
        name=f"dilated_attention_{dil}",
    )(qkv_g, qkv_g, qkv_g, qkv_g, qkv_g)


def _attn_out_kernel(o0_ref, o1_ref, o2_ref, l0_ref, l1_ref, l2_ref, x_ref, mod_ref, wout_ref, e_ref,
                     g2_ref, wr_ref, br_ref, x1_ref, h2_ref, lg_ref, o_scr, l_scr):
    ts = x_ref.shape[1]
    nc = ATTN_OUT_DIM // V7X_LANES
    for g, (o_ref, l_ref) in enumerate(((o0_ref, l0_ref), (o1_ref, l1_ref), (o2_ref, l2_ref))):
        dil = DILATED_GROUPS[g][1]
        for r in range(dil):
            rows = pl.ds(r, ts // dil, stride=dil) if dil > 1 else slice(None)
            o_r = o_ref[0, r].astype(F32)
            for cc in range(nc):
                o_scr[g, cc, rows, :] = o_r[:, cc * V7X_LANES:(cc + 1) * V7X_LANES]
            l_scr[g, rows, :] = l_ref[0, r]
    l0, l1, l2 = l_scr[0], l_scr[1], l_scr[2]
    m = jnp.maximum(jnp.maximum(l0, l1), l2)
    es = (jnp.exp(l0 - m), jnp.exp(l1 - m), jnp.exp(l2 - m))
    den = es[0] + es[1] + es[2]
    merged = jnp.zeros((ts, ATTN_OUT_DIM), F32)
    for g in range(N_DIL_GROUPS):
        w = es[g] / den
        w_hi = w.astype(BF16)
        w_lo = (w - w_hi.astype(F32)).astype(BF16)
        wfull = (jnp.dot(w_hi, e_ref[...], preferred_element_type=F32)
                 + jnp.dot(w_lo, e_ref[...], preferred_element_type=F32))
        merged = merged + wfull * jnp.concatenate([o_scr[g, cc] for cc in range(nc)], axis=1)
    y = jnp.dot(merged.astype(BF16), wout_ref[...], preferred_element_type=F32)
    mod = mod_ref[0]
    x1 = x_ref[0] + mod[2:3] * y
    x1_ref[0] = x1
    _ffn_input_and_logits(x1, mod, g2_ref, wr_ref, br_ref, h2_ref, lg_ref)


def _attn_out(outs, lses, x, mod, w_out, g_ffn, wr, br):
    b, s, d = x.shape
    ts = min(ROW_TILE, s)
    nst = s // ts
    nj = d // (2 * V7X_LANES)
    const = lambda bi, si: (0, 0)
    tok = lambda bi, si: (bi * nst + si, 0)
    stream = lambda bi, si: (bi, 0, si, 0)
    expand = (jnp.arange(V7X_LANES)[:, None] == jnp.arange(ATTN_OUT_DIM)[None, :] // HEAD_DIM).astype(BF16)
    return pl.pallas_call(
        _attn_out_kernel,
        out_shape=(jax.ShapeDtypeStruct((b, s, d), F32),
                   jax.ShapeDtypeStruct((b * s * nj, V7X_LANES), jnp.uint32),
                   jax.ShapeDtypeStruct((b * s, V7X_LANES), F32)),
        grid=(b, nst),
        in_specs=[pl.BlockSpec((1, dil, ts // dil, ATTN_OUT_DIM), stream) for _, dil in DILATED_GROUPS]
                 + [pl.BlockSpec((1, dil, ts // dil, V7X_LANES), stream) for _, dil in DILATED_GROUPS]
                 + [pl.BlockSpec((1, ts, d), lambda bi, si: (bi, si, 0)),
                    pl.BlockSpec((1, 6, d), lambda bi, si: (bi, 0, 0)),
                    pl.BlockSpec((ATTN_OUT_DIM, d), const),
                    pl.BlockSpec((V7X_LANES, ATTN_OUT_DIM), const),
                    pl.BlockSpec((1, d), const),
                    pl.BlockSpec((d, V7X_LANES), const),
                    pl.BlockSpec((1, V7X_LANES), const)],
        out_specs=(pl.BlockSpec((1, ts, d), lambda bi, si: (bi, si, 0)),
                   pl.BlockSpec((ts * nj, V7X_LANES), tok),
                   pl.BlockSpec((ts, V7X_LANES), tok)),
        scratch_shapes=[pltpu.VMEM((N_DIL_GROUPS, ATTN_OUT_DIM // V7X_LANES, ts, V7X_LANES), F32),
                        pltpu.VMEM((N_DIL_GROUPS, ts, V7X_LANES), F32)],
        compiler_params=_params(2),
        name="attn_out",
    )(*outs, *lses, x, mod, w_out, expand, g_ffn, wr, br)


def _router_params(w_grp, b_grp, w_exp, b_exp):
    d = w_grp.shape[0]
    w_e = jnp.transpose(w_exp, (1, 0, 2)).reshape(d, N_EXPERTS)
    pad1 = ROUTER_LANE_OFFSET - N_GROUPS
    pad2 = V7X_LANES - ROUTER_LANE_OFFSET - N_EXPERTS
    wr = jnp.concatenate([w_grp, jnp.zeros((d, pad1), F32), w_e, jnp.zeros((d, pad2), F32)], axis=1)
    br = jnp.concatenate([b_grp, jnp.zeros((pad1,), F32), b_exp.reshape(-1), jnp.zeros((pad2,), F32)])
    return wr.astype(BF16), br.reshape(1, V7X_LANES).astype(F32)


def kernel(x, c, norm_mix_g, norm_ffn_g, ada_w, ada_b, conv_in_w, conv_w, conv_out_w, attn_in_w,
           attn_out_w, router_grp_w, router_grp_b, router_exp_w, router_exp_b, exp_gate_w, exp_up_w,
           exp_down_w, final_norm_g):
    b, s, d = x.shape
    depth = ada_w.shape[0]
    assert depth == 2 and s % (DILATED_GROUPS[-1][1] * ATTN_SPAN) == 0 and d % (2 * V7X_LANES) == 0
    mod = _ada_modulation(c, ada_w, ada_b).reshape(depth, b, 6, d)
    g_fin = final_norm_g.reshape(1, d)

    wr, br = _router_params(router_grp_w[0], router_grp_b[0], router_exp_w[0], router_exp_b[0])
    x1, h2, logits = _conv_mixer(x, mod[0], norm_mix_g[0:1], conv_in_w[0].astype(BF16), conv_w[0],
                                 conv_out_w[0].astype(BF16), norm_ffn_g[0:1], wr, br)
    row_t, plan, ys = _moe_experts(h2, logits, exp_gate_w, exp_up_w, exp_down_w, layer=0)

    wr, br = _router_params(router_grp_w[1], router_grp_b[1], router_exp_w[1], router_exp_b[1])
    x2, *qkv = _combine_qkv_proj(row_t, x1, plan, ys, mod[0], mod[1], norm_mix_g[1:2],
                                 attn_in_w[0].astype(BF16))
    outs, lses = zip(*[_dilated_attention(qkv_g) for qkv_g in qkv])
    x3, h2, logits = _attn_out(outs, lses, x2, mod[1], attn_out_w[0].astype(BF16), norm_ffn_g[1:2],
                               wr, br)
    row_t, plan, ys = _moe_experts(h2, logits, exp_gate_w, exp_up_w, exp_down_w, layer=1)
    return _combine_final(row_t, x3, plan, mod[1], g_fin, ys)
```

```python
import functools

import jax
import jax.numpy as jnp
from jax import lax
from jax.experimental import pallas as pl
from jax.experimental.pallas import tpu as pltpu

CONV_WIDTH = 3
DILATED_GROUPS = ((128, 1), (512, 4), (2048, 16))
N_DIL_GROUPS = len(DILATED_GROUPS)
ATTN_HEADS = 8
HEAD_DIM = 64
ATTN_OUT_DIM = ATTN_HEADS * HEAD_DIM
ATTN_IN_DIM = N_DIL_GROUPS * 3 * ATTN_OUT_DIM
ROT_DIM = HEAD_DIM // 4
ROPE_THETA = 500000.0
N_GROUPS = 4
EXPERTS_PER_GROUP = 8
N_EXPERTS = N_GROUPS * EXPERTS_PER_GROUP
NORM_EPS = 1e-6
ATTN_SPAN = 128

V7X_LANES = 128
V7X_VMEM_LIMIT_BYTES = 56 * 1024 * 1024

ROW_TILE = 512
CONV_COL_BLOCK = 256
MIXER_ROW_CHUNKS = 2
ROUTE_TILE = 1024
MOE_BLOCK = 512
EXPERT_ROW_CHUNKS = 2
ATTN_Q_TILE = 512
ATTN_ITEM_GROUP = 16
ROUTER_LANE_OFFSET = 8
DMA_ROW_TILE = 1024

F32 = jnp.float32
BF16 = jnp.bfloat16


def _params(n_axes):
    return pltpu.CompilerParams(dimension_semantics=("arbitrary",) * n_axes,
                                vmem_limit_bytes=V7X_VMEM_LIMIT_BYTES)


def _norm_mod(x, g, shift, scale):
    ms = jnp.mean(x * x, axis=-1, keepdims=True)
    y = x * lax.rsqrt(ms + NORM_EPS) * g
    return y * (1.0 + scale) + shift


def _pack_bf16_pair(a, b):
    return pltpu.pack_elementwise([a, b], packed_dtype=BF16)


def _unpack_bf16_pair(w):
    a = pltpu.unpack_elementwise(w, index=0, packed_dtype=BF16, unpacked_dtype=F32)
    b = pltpu.unpack_elementwise(w, index=1, packed_dtype=BF16, unpacked_dtype=F32)
    return a, b


def _store_row_groups(ref, x, first=0):
    rows, d = x.shape
    nj = d // (2 * V7X_LANES)
    for j in range(nj):
        lo = x[:, j * V7X_LANES:(j + 1) * V7X_LANES]
        hi = x[:, d // 2 + j * V7X_LANES:d // 2 + (j + 1) * V7X_LANES]
        ref[pl.ds(first * nj + j, rows, stride=nj), :] = _pack_bf16_pair(lo, hi)


def _load_row_groups(ref, rows, nj, first=0):
    los, his = [], []
    for j in range(nj):
        lo, hi = _unpack_bf16_pair(ref[pl.ds(first * nj + j, rows, stride=nj), :])
        los.append(lo)
        his.append(hi)
    return los + his


def _slot_rows_block(ts):
    return (2 * ts // V7X_LANES, V7X_LANES)


def _ffn_input_and_logits(x1, mod, g2_ref, wr_ref, br_ref, h2_ref, lg_ref, first=0):
    rows = x1.shape[0]
    h2 = _norm_mod(x1, g2_ref[...], mod[3:4], mod[4:5])
    _store_row_groups(h2_ref, h2, first)
    lg_ref[first:first + rows, :] = (jnp.dot(h2.astype(BF16), wr_ref[...], preferred_element_type=F32)
                                     + br_ref[...])


def _ada_kernel(c_ref, w_ref, b_ref, o_ref):
    c = c_ref[...]
    ca = c * jax.nn.sigmoid(c)
    o_ref[0] = jnp.dot(ca, w_ref[0], preferred_element_type=F32,
                       precision=lax.Precision.HIGHEST) + b_ref[0]


def _ada_modulation(c, ada_w, ada_b):
    depth, d, n = ada_w.shape
    b = c.shape[0]
    tn = d
    return pl.pallas_call(
        _ada_kernel,
        out_shape=jax.ShapeDtypeStruct((depth, b, n), F32),
        grid=(depth, n // tn),
        in_specs=[pl.BlockSpec((b, d), lambda i, j: (0, 0)),
                  pl.BlockSpec((1, d, tn), lambda i, j: (i, 0, j)),
                  pl.BlockSpec((1, 1, tn), lambda i, j: (i, 0, j))],
        out_specs=pl.BlockSpec((1, b, tn), lambda i, j: (i, 0, j)),
        compiler_params=_params(2),
        name="ada_modulation",
    )(c, ada_w, ada_b.reshape(depth, 1, n))


def _conv_mixer_kernel(x_ref, mod_ref, g_ref, win_ref, cw_ref, wout_ref, g2_ref, wr_ref, br_ref,
                       x1_ref, h2_ref, lg_ref, carry_ref, v_ref, *, cb):
    ts, d = x_ref.shape[1], x_ref.shape[2]

    @pl.when(pl.program_id(1) == 0)
    def _():
        carry_ref[...] = jnp.zeros_like(carry_ref)

    mod = mod_ref[0]
    nr = MIXER_ROW_CHUNKS if ts % (8 * MIXER_ROW_CHUNKS) == 0 else 1
    hr = ts // nr
    xs = [x_ref[0, r * hr:(r + 1) * hr, :] for r in range(nr)]
    hs = [_norm_mod(x, g_ref[...], mod[0:1], mod[1:2]).astype(BF16) for x in xs]
    row = lax.broadcasted_iota(jnp.int32, (hr, cb), 0)
    tails = [carry_ref[:, j * cb:(j + 1) * cb] for j in range(d // cb)]
    for r in range(nr):
        for j in range(d // cb):
            lo = j * cb
            bj = jnp.dot(hs[r], win_ref[:, lo:lo + cb], preferred_element_type=F32)
            cj = jnp.dot(hs[r], win_ref[:, d + lo:d + lo + cb], preferred_element_type=F32)
            uj = jnp.dot(hs[r], win_ref[:, 2 * d + lo:2 * d + lo + cb], preferred_element_type=F32)
            z = cj * uj
            prev = tails[j]
            z1 = jnp.where(row == 0, prev[7:8], pltpu.roll(z, 1, 0))
            z2 = jnp.where(row == 0, prev[6:7], jnp.where(row == 1, prev[7:8], pltpu.roll(z, 2, 0)))
            zc = cw_ref[0:1, lo:lo + cb] * z2 + cw_ref[1:2, lo:lo + cb] * z1 + cw_ref[2:3, lo:lo + cb] * z
            tails[j] = z[hr - 8:hr]
            v_ref[r * hr:(r + 1) * hr, lo:lo + cb] = (bj * zc).astype(BF16)
    for j in range(d // cb):
        carry_ref[:, j * cb:(j + 1) * cb] = tails[j]
    for r in range(nr):
        y = jnp.dot(v_ref[r * hr:(r + 1) * hr, :], wout_ref[...], preferred_element_type=F32)
        x1 = xs[r] + mod[2:3] * y
        x1_ref[0, r * hr:(r + 1) * hr, :] = x1
        _ffn_input_and_logits(x1, mod, g2_ref, wr_ref, br_ref, h2_ref, lg_ref, r * hr)


def _conv_mixer(x, mod, g_mix, w_in, conv_w, w_out, g_ffn, wr, br):
    b, s, d = x.shape
    ts = min(ROW_TILE, s)
    cb = min(CONV_COL_BLOCK, d)
    nst = s // ts
    nj = d // (2 * V7X_LANES)
    const = lambda bi, si: (0, 0)
    return pl.pallas_call(
        functools.partial(_conv_mixer_kernel, cb=cb),
        out_shape=(jax.ShapeDtypeStruct((b, s, d), F32),
                   jax.ShapeDtypeStruct((b * s * nj, V7X_LANES), jnp.uint32),
                   jax.ShapeDtypeStruct((b * s, V7X_LANES), F32)),
        grid=(b, nst),
        in_specs=[pl.BlockSpec((1, ts, d), lambda bi, si: (bi, si, 0)),
                  pl.BlockSpec((1, 6, d), lambda bi, si: (bi, 0, 0)),
                  pl.BlockSpec((1, d), const),
                  pl.BlockSpec((d, 3 * d), const),
                  pl.BlockSpec((CONV_WIDTH, d), const),
                  pl.BlockSpec((d, d), const),
                  pl.BlockSpec((1, d), const),
                  pl.BlockSpec((d, V7X_LANES), const),
                  pl.BlockSpec((1, V7X_LANES), const)],
        out_specs=(pl.BlockSpec((1, ts, d), lambda bi, si: (bi, si, 0)),
                   pl.BlockSpec((ts * nj, V7X_LANES), lambda bi, si: (bi * nst + si, 0)),
                   pl.BlockSpec((ts, V7X_LANES), lambda bi, si: (bi * nst + si, 0))),
        scratch_shapes=[pltpu.VMEM((8, d), F32), pltpu.VMEM((ts, d), BF16)],
        compiler_params=_params(2),
        name="conv_mixer",
    )(x, mod, g_mix, w_in, conv_w, w_out, g_ffn, wr, br)


def _route_kernel(lg_ref, tri_ref, plan_ref, cnt_ref, run_ref, pst_ref, *, blk):
    ph, i = pl.program_id(0), pl.program_id(1)
    tr, lanes = lg_ref.shape
    epg = EXPERTS_PER_GROUP

    @pl.when((ph == 0) & (i == 0))
    def _():
        run_ref[...] = jnp.zeros_like(run_ref)

    lt = lg_ref[...].T
    row8 = lax.broadcasted_iota(jnp.int32, (epg, tr), 0).astype(F32)
    neg = jnp.float32(-jnp.inf)
    big = jnp.float32(epg)
    gmask = row8 < N_GROUPS
    gl = jnp.where(gmask, lt[0:epg], neg)
    mg = jnp.max(gl, axis=0, keepdims=True)
    gidx = jnp.min(jnp.where(gl == mg, row8, big), axis=0, keepdims=True)
    den_g = jnp.sum(jnp.where(gmask, jnp.exp(lt[0:epg] - mg), 0.0), axis=0, keepdims=True)
    grp_p = 1.0 / den_g
    sel = lt[ROUTER_LANE_OFFSET:ROUTER_LANE_OFFSET + epg]
    for g in range(1, N_GROUPS):
        sel = jnp.where(gidx == g, lt[ROUTER_LANE_OFFSET + g * epg:ROUTER_LANE_OFFSET + (g + 1) * epg], sel)
    l1 = jnp.max(sel, axis=0, keepdims=True)
    i1 = jnp.min(jnp.where(sel == l1, row8, big), axis=0, keepdims=True)
    sel2 = jnp.where(row8 == i1, neg, sel)
    l2 = jnp.max(sel2, axis=0, keepdims=True)
    i2 = jnp.min(jnp.where(sel2 == l2, row8, big), axis=0, keepdims=True)
    r = jnp.exp(l2 - l1)
    w1 = grp_p / (1.0 + r)
    w2 = grp_p * r / (1.0 + r)
    e1 = gidx * epg + i1
    e2 = gidx * epg + i2
    row_e = lax.broadcasted_iota(jnp.int32, (N_EXPERTS, tr), 0).astype(F32)
    oh1 = (row_e == e1).astype(F32)
    oh2 = (row_e == e2).astype(F32)
    tot1 = jnp.sum(oh1, axis=1, keepdims=True)
    tot2 = jnp.sum(oh2, axis=1, keepdims=True)
    run = run_ref[:, 0:1]

    @pl.when(ph == 0)
    def _():
        new_run = run + tot1 + tot2
        run_ref[...] = jnp.broadcast_to(new_run, run_ref.shape)

        @pl.when(i == pl.num_programs(1) - 1)
        def _():
            cnt = jnp.broadcast_to(new_run, run_ref.shape)
            padded = jnp.floor((cnt + (blk - 1)) / blk) * blk
            rid = lax.broadcasted_iota(jnp.int32, cnt.shape, 0)
            csum = padded
            k = 1
            while k < N_EXPERTS:
                csum = csum + jnp.where(rid >= k, pltpu.roll(csum, k, 0), 0.0)
                k *= 2
            pst_ref[...] = csum - padded
            run_ref[...] = jnp.zeros_like(run_ref)
            lid = lax.broadcasted_iota(jnp.int32, cnt.shape, 1)
            cnt_ref[...] = jnp.where(lid == 0, cnt, jnp.where(lid == 1, csum, jnp.where(lid == 2, padded, 0.0)))

    @pl.when(ph == 1)
    def _():
        base = pst_ref[:, 0:1] + run
        oh = jnp.concatenate([oh1, oh2], axis=0).astype(BF16)
        off = jnp.concatenate([base, base + tot1], axis=0)
        pos1, pos2 = [], []
        for c in range(tr // lanes):
            ohc = oh[:, c * lanes:(c + 1) * lanes]
            before = jnp.dot(ohc, tri_ref[...], preferred_element_type=F32) + off
            ohf = ohc.astype(F32)
            pos1.append(jnp.sum(ohf[:N_EXPERTS] * before[:N_EXPERTS], axis=0, keepdims=True))
            pos2.append(jnp.sum(ohf[N_EXPERTS:] * before[N_EXPERTS:], axis=0, keepdims=True))
            off = off + jnp.sum(ohf, axis=1, keepdims=True)
        run_ref[...] = jnp.broadcast_to(run + tot1 + tot2, run_ref.shape)
        pos1 = jnp.concatenate(pos1, axis=1)
        pos2 = jnp.concatenate(pos2, axis=1)
        head = jnp.where(row8 == 0, pos1, jnp.where(row8 == 1, pos2,
                         jnp.where(row8 == 2, w1, jnp.where(row8 == 3, w2, 0.0))))
        plan_t = jnp.concatenate([head, jnp.zeros((lanes - epg, tr), F32)], axis=0)
        plan_ref[...] = plan_t.T


def _route(logits):
    t, lanes = logits.shape
    tr = min(ROUTE_TILE, t)
    nt = t // tr
    tri = (jnp.arange(lanes)[:, None] < jnp.arange(lanes)[None, :]).astype(BF16)
    return pl.pallas_call(
        functools.partial(_route_kernel, blk=MOE_BLOCK),
        out_shape=(jax.ShapeDtypeStruct((t, lanes), F32),
                   jax.ShapeDtypeStruct((N_EXPERTS, lanes), F32)),
        grid=(2, nt),
        in_specs=[pl.BlockSpec((tr, lanes), lambda ph, i: (i, 0)),
                  pl.BlockSpec((lanes, lanes), lambda ph, i: (0, 0))],
        out_specs=(pl.BlockSpec((tr, lanes), lambda ph, i: (i * ph, 0)),
                   pl.BlockSpec((N_EXPERTS, lanes), lambda ph, i: (0, 0))),
        scratch_shapes=[pltpu.VMEM((N_EXPERTS, lanes), F32), pltpu.VMEM((N_EXPERTS, lanes), F32)],
        compiler_params=_params(2),
        name="route",
    )(logits, tri)


def _dispatch_kernel(last_ref, row_ref, h2_ref, xs_hbm, zero_buf, sem, zero_sem, *, nj):
    ts = h2_ref.shape[0] // nj
    blk_rows = zero_buf.shape[0]

    @pl.when(pl.program_id(0) == 0)
    def _():
        z = jnp.zeros(zero_buf.shape, F32)
        zero_buf[...] = _pack_bf16_pair(z, z)

        def blank(e):
            dst = xs_hbm.at[pl.ds(pl.multiple_of(last_ref[e], nj), blk_rows)]
            return pltpu.make_async_copy(zero_buf, dst, zero_sem)

        def zstart(e, carry):
            @pl.when(last_ref[e] >= 0)
            def _():
                blank(e).start()
            return carry

        def zwait(e, carry):
            @pl.when(last_ref[e] >= 0)
            def _():
                blank(e).wait()
            return carry

        def tail(bi):
            dst = xs_hbm.at[pl.ds(pl.multiple_of(bi * blk_rows, blk_rows), blk_rows)]
            return pltpu.make_async_copy(zero_buf, dst, zero_sem)

        def tstart(bi, carry):
            tail(bi).start()
            return carry

        def twait(bi, carry):
            tail(bi).wait()
            return carry

        n_blocks = xs_hbm.shape[0] // blk_rows
        lax.fori_loop(0, N_EXPERTS, zstart, 0)
        lax.fori_loop(last_ref[N_EXPERTS], n_blocks, tstart, 0)
        lax.fori_loop(0, N_EXPERTS, zwait, 0)
        lax.fori_loop(last_ref[N_EXPERTS], n_blocks, twait, 0)

    chunk = row_ref.shape[1]

    def start(c, carry):
        t0 = pl.multiple_of(c * chunk, chunk)
        for u in range(chunk):
            src = h2_ref.at[pl.ds(pl.multiple_of((t0 + u) * nj, nj), nj)]
            for k in range(2):
                pltpu.make_async_copy(src, xs_hbm.at[pl.ds(pl.multiple_of(row_ref[2 * c + k, u], nj), nj)],
                                      sem).start(priority=k)
        return carry

    lax.fori_loop(0, ts // chunk, start, 0)
    for _ in range(2):
        pltpu.make_async_copy(h2_ref, xs_hbm.at[pl.ds(0, ts * nj)], sem).wait()


def _dispatch(last_row, row_t, h2, n_rows, nj):
    t = h2.shape[0] // nj
    ts = min(DMA_ROW_TILE, t)
    grid_spec = pltpu.PrefetchScalarGridSpec(
        num_scalar_prefetch=1,
        grid=(t // ts,),
        in_specs=[pl.BlockSpec(_slot_rows_block(ts), lambda i, last: (i, 0), memory_space=pltpu.SMEM),
                  pl.BlockSpec((ts * nj, V7X_LANES), lambda i, last: (i, 0))],
        out_specs=pl.BlockSpec(memory_space=pl.ANY),
        scratch_shapes=[pltpu.VMEM((MOE_BLOCK * nj, V7X_LANES), h2.dtype),
                        pltpu.SemaphoreType.DMA(()), pltpu.SemaphoreType.DMA(())])
    return pl.pallas_call(
        functools.partial(_dispatch_kernel, nj=nj),
        out_shape=jax.ShapeDtypeStruct((n_rows * nj, V7X_LANES), h2.dtype),
        grid_spec=grid_spec,
        compiler_params=_params(1),
        name="dispatch",
    )(last_row, row_t, h2)


def _experts_kernel(be_ref, nact_ref, xs_ref, wg_ref, wu_ref, wd_ref, ys_ref, wgb, wub, wdb, *, nj):
    i = pl.program_id(0)
    blk = xs_ref.shape[0] // nj

    @pl.when(i < nact_ref[0])
    def _():
        prev = be_ref[jnp.maximum(i - 1, 0)]

        @pl.when((i == 0) | (be_ref[i] != prev))
        def _():
            wgb[...] = wg_ref[0, 0].astype(BF16)
            wub[...] = wu_ref[0, 0].astype(BF16)
            wdb[...] = wd_ref[0, 0].astype(BF16)

        nchunk = EXPERT_ROW_CHUNKS if blk % (8 * EXPERT_ROW_CHUNKS) == 0 else 1
        rows = blk // nchunk
        xs = [jnp.concatenate([p.astype(BF16) for p in _load_row_groups(xs_ref, rows, nj, c * rows)], axis=1)
              for c in range(nchunk)]
        gs = [jnp.dot(x, wgb[...], preferred_element_type=F32) for x in xs]
        us = [jnp.dot(x, wub[...], preferred_element_type=F32) for x in xs]
        acts = [(g * jax.nn.sigmoid(g) * u).astype(BF16) for g, u in zip(gs, us)]
        ys = [jnp.dot(a, wdb[...], preferred_element_type=F32) for a in acts]
        for c, y in enumerate(ys):
            _store_row_groups(ys_ref, y, c * rows)

    @pl.when(i >= nact_ref[0])
    def _():
        z = jnp.zeros(ys_ref.shape, F32)
        ys_ref[...] = _pack_bf16_pair(z, z)


def _experts(block_e, n_active, xs, w_gate, w_up, w_down, layer):
    _, _, d, de = w_gate.shape
    nj = d // (2 * V7X_LANES)
    n_blocks = xs.shape[0] // (MOE_BLOCK * nj)
    grid_spec = pltpu.PrefetchScalarGridSpec(
        num_scalar_prefetch=2,
        grid=(n_blocks,),
        in_specs=[pl.BlockSpec((MOE_BLOCK * nj, V7X_LANES), lambda i, be, na: (i, 0)),
                  pl.BlockSpec((1, 1, d, de), lambda i, be, na: (layer, be[i], 0, 0)),
                  pl.BlockSpec((1, 1, d, de), lambda i, be, na: (layer, be[i], 0, 0)),
                  pl.BlockSpec((1, 1, de, d), lambda i, be, na: (layer, be[i], 0, 0))],
        out_specs=pl.BlockSpec((MOE_BLOCK * nj, V7X_LANES), lambda i, be, na: (i, 0)),
        scratch_shapes=[pltpu.VMEM((d, de), BF16), pltpu.VMEM((d, de), BF16),
                        pltpu.VMEM((de, d), BF16)])
    return pl.pallas_call(
        functools.partial(_experts_kernel, nj=nj),
        out_shape=jax.ShapeDtypeStruct(xs.shape, jnp.uint32),
        grid_spec=grid_spec,
        compiler_params=_params(1),
        name="experts",
    )(block_e, n_active, xs, w_gate, w_up, w_down)


def _start_row_gathers(row_ref, ys_hbm, buf, sem, ts, nj):
    chunk = row_ref.shape[1]

    def start(c, carry):
        t0 = pl.multiple_of(c * chunk, chunk)
        for u in range(chunk):
            dst = pl.ds(pl.multiple_of((t0 + u) * nj, nj), nj)
            for k in range(2):
                pltpu.make_async_copy(ys_hbm.at[pl.ds(pl.multiple_of(row_ref[2 * c + k, u], nj), nj)],
                                      buf.at[k, dst], sem).start(priority=k)
        return carry

    lax.fori_loop(0, ts // chunk, start, 0)


def _wait_row_gathers(ys_hbm, buf, sem, ts, nj):
    for k in range(2):
        pltpu.make_async_copy(ys_hbm.at[pl.ds(0, ts * nj)], buf.at[k], sem).wait()


def _weighted_expert_rows(buf, plan_ref, ts, nj):
    w1 = plan_ref[:, 2:3]
    w2 = plan_ref[:, 3:4]
    p1 = _load_row_groups(buf.at[0], ts, nj)
    p2 = _load_row_groups(buf.at[1], ts, nj)
    return jnp.concatenate([w1 * a + w2 * b for a, b in zip(p1, p2)], axis=1)


def _gathered_expert_rows(step, n_steps, rcur_ref, rnext_ref, plan_ref, ys_hbm, buf, sems, ts, nj):
    slot = lax.rem(step, 2)

    @pl.when(step == 0)
    def _():
        _start_row_gathers(rcur_ref, ys_hbm, buf.at[0], sems.at[0], ts, nj)

    @pl.when(step + 1 < n_steps)
    def _():
        _start_row_gathers(rnext_ref, ys_hbm, buf.at[1 - slot], sems.at[1 - slot], ts, nj)

    _wait_row_gathers(ys_hbm, buf.at[slot], sems.at[slot], ts, nj)
    return _weighted_expert_rows(buf.at[slot], plan_ref, ts, nj)


def _combine_kernel(rcur_ref, rnext_ref, x_ref, plan_ref, mod_ref, gfin_ref, ys_hbm, out_ref, buf, sems, *, nj):
    ts = x_ref.shape[1]
    moe = _gathered_expert_rows(pl.program_id(0), pl.num_programs(0), rcur_ref, rnext_ref, plan_ref,
                                ys_hbm, buf, sems, ts, nj)
    x2 = x_ref[0] + mod_ref[0][5:6] * moe
    ms = jnp.mean(x2 * x2, axis=-1, keepdims=True)
    out_ref[0] = x2 * lax.rsqrt(ms + NORM_EPS) * gfin_ref[...]


def _combine_final(row_t, x1, plan, mod, g_final, ys):
    b, s, d = x1.shape
    ts = min(DMA_ROW_TILE, s)
    nst = s // ts
    n_tiles = b * nst
    nj = d // (2 * V7X_LANES)
    return pl.pallas_call(
        functools.partial(_combine_kernel, nj=nj),
        out_shape=jax.ShapeDtypeStruct((b, s, d), F32),
        grid=(n_tiles,),
        in_specs=[pl.BlockSpec(_slot_rows_block(ts), lambda i: (i, 0), memory_space=pltpu.SMEM),
                  pl.BlockSpec(_slot_rows_block(ts), lambda i: (jnp.minimum(i + 1, n_tiles - 1), 0),
                               memory_space=pltpu.SMEM),
                  pl.BlockSpec((1, ts, d), lambda i: (i // nst, i % nst, 0)),
                  pl.BlockSpec((ts, plan.shape[1]), lambda i: (i, 0)),
                  pl.BlockSpec((1, 6, d), lambda i: (i // nst, 0, 0)),
                  pl.BlockSpec((1, d), lambda i: (0, 0)),
                  pl.BlockSpec(memory_space=pl.ANY)],
        out_specs=pl.BlockSpec((1, ts, d), lambda i: (i // nst, i % nst, 0)),
        scratch_shapes=[pltpu.VMEM((2, 2, ts * nj, V7X_LANES), jnp.uint32),
                        pltpu.SemaphoreType.DMA((2,))],
        compiler_params=_params(1),
        name="combine_final",
    )(row_t, row_t, x1, plan, mod, g_final, ys)


def _moe_experts(h2, logits, w_gate, w_up, w_down, *, layer):
    t = logits.shape[0]
    nj = h2.shape[0] // t
    plan, cnt = _route(logits)
    n_blocks = -(-(t * 2) // MOE_BLOCK) + N_EXPERTS
    seg_end, seg_rows = cnt[:, 1], cnt[:, 2]
    pends = (seg_end * (1.0 / MOE_BLOCK)).astype(jnp.int32)
    block_e = jnp.minimum(jnp.sum(pends[None, :] <= jnp.arange(n_blocks)[:, None], axis=1),
                          N_EXPERTS - 1).astype(jnp.int32)
    n_active = pends[-1:]
    last_row = jnp.where(seg_rows > 0, (seg_end - MOE_BLOCK) * nj, -1.0).astype(jnp.int32)
    last_row = jnp.concatenate([last_row, n_active])
    row_t = (plan[:, 0:2] * nj).astype(jnp.int32).reshape(t // V7X_LANES, V7X_LANES, 2)
    row_t = jnp.transpose(row_t, (0, 2, 1)).reshape(2 * t // V7X_LANES, V7X_LANES)
    xs = _dispatch(last_row, row_t, h2, n_blocks * MOE_BLOCK, nj)
    ys = _experts(block_e, n_active, xs, w_gate, w_up, w_down, layer)
    return row_t, plan, ys


def _qkv_kernel(rcur_ref, rnext_ref, x_ref, plan_ref, mod0_ref, mod_ref, g_ref, w_ref, *refs, nj):
    tabs, ys_hbm, x2_ref, (o0_ref, o1_ref, o2_ref) = refs[:9], refs[9], refs[10], refs[11:14]
    scr, buf, sems = refs[14:17]
    ts, d = x_ref.shape[1], x_ref.shape[2]
    step = pl.program_id(0) * pl.num_programs(1) + pl.program_id(1)
    n_steps = pl.num_programs(0) * pl.num_programs(1)
    moe = _gathered_expert_rows(step, n_steps, rcur_ref, rnext_ref, plan_ref, ys_hbm, buf, sems, ts, nj)
    x2 = x_ref[0] + mod0_ref[0][5:6] * moe
    x2_ref[0] = x2
    mod = mod_ref[0]
    hn = _norm_mod(x2, g_ref[...], mod[0:1], mod[1:2])
    wd = ATTN_OUT_DIM
    rep = wd // V7X_LANES
    nc = d // V7X_LANES
    for g, o_ref in enumerate((o0_ref, o1_ref, o2_ref)):
        dil = DILATED_GROUPS[g][1]
        n = ts // dil
        if dil == 1:
            h = hn.astype(BF16)
        else:
            if g == 1:
                for cc in range(nc):
                    scr[cc] = hn[:, cc * V7X_LANES:(cc + 1) * V7X_LANES]
            h = jnp.concatenate(
                [jnp.concatenate([scr[cc, pl.ds(r, n, stride=dil), :] for cc in range(nc)], axis=1)
                 for r in range(dil)], axis=0).astype(BF16)
        cos, sa, sb = (jnp.tile(t[...], (1, rep)) for t in tabs[3 * g:3 * g + 3])
        for j in range(3):
            slab = g * 3 + j
            acc = jnp.dot(h, w_ref[:, slab * wd:(slab + 1) * wd], preferred_element_type=F32)
            if j != 2:
                acc = (acc * cos + pltpu.roll(acc, ROT_DIM // 2, 1) * sa
                       + pltpu.roll(acc, wd - ROT_DIM // 2, 1) * sb)
            acc = acc.astype(BF16)
            for r in range(dil):
                o_ref[0, r, :, j * wd:(j + 1) * wd] = acc[r * n:(r + 1) * n]


def _rope_tables(s, ts):
    half = ROT_DIM // 2
    pos = jnp.arange(s, dtype=F32)
    inv_freq = jnp.power(jnp.float32(ROPE_THETA), -jnp.arange(0, ROT_DIM, 2, dtype=F32) / ROT_DIM)
    ang = pos[:, None] * inv_freq[None, :]
    cos, sin = jnp.cos(ang), jnp.sin(ang)
    m = jnp.arange(V7X_LANES) % HEAD_DIM
    cos_l = jnp.where(m[None, :] < ROT_DIM, cos[:, m % half], 1.0)
    sa_l = jnp.where((m[None, :] >= half) & (m[None, :] < ROT_DIM), sin[:, m % half], 0.0)
    sb_l = jnp.where(m[None, :] < half, -sin[:, m % half], 0.0)
    tabs = []
    for _, dil in DILATED_GROUPS:
        for t in (cos_l, sa_l, sb_l):
            t = t.astype(F32).reshape(s // ts, ts // dil, dil, V7X_LANES)
            tabs.append(jnp.transpose(t, (0, 2, 1, 3)).reshape(s, V7X_LANES))
    return tabs


def _combine_qkv_proj(row_t, x1, plan, ys, mod0, mod, g_mix, w_in):
    b, s, d = x1.shape
    ts = min(ROW_TILE, s)
    nst = s // ts
    nj = d // (2 * V7X_LANES)
    const = lambda si, bi: (0, 0)
    tab = pl.BlockSpec((ts, V7X_LANES), lambda si, bi: (si, 0))
    wd3 = 3 * ATTN_OUT_DIM

    def next_tile(si, bi):
        nb = lax.rem(bi + 1, b)
        ns = jnp.minimum(si + lax.div(bi + 1, b), nst - 1)
        return (nb * nst + ns, 0)

    return pl.pallas_call(
        functools.partial(_qkv_kernel, nj=nj),
        out_shape=(jax.ShapeDtypeStruct((b, s, d), F32),)
                  + tuple(jax.ShapeDtypeStruct((b, dil, s // dil, wd3), BF16) for _, dil in DILATED_GROUPS),
        grid=(nst, b),
        in_specs=[pl.BlockSpec(_slot_rows_block(ts), lambda si, bi: (bi * nst + si, 0),
                               memory_space=pltpu.SMEM),
                  pl.BlockSpec(_slot_rows_block(ts), next_tile, memory_space=pltpu.SMEM),
                  pl.BlockSpec((1, ts, d), lambda si, bi: (bi, si, 0)),
                  pl.BlockSpec((ts, plan.shape[1]), lambda si, bi: (bi * nst + si, 0)),
                  pl.BlockSpec((1, 6, d), lambda si, bi: (bi, 0, 0)),
                  pl.BlockSpec((1, 6, d), lambda si, bi: (bi, 0, 0)),
                  pl.BlockSpec((1, d), const),
                  pl.BlockSpec((d, ATTN_IN_DIM), const)] + [tab] * 9
                 + [pl.BlockSpec(memory_space=pl.ANY)],
        out_specs=(pl.BlockSpec((1, ts, d), lambda si, bi: (bi, si, 0)),)
                  + tuple(pl.BlockSpec((1, dil, ts // dil, wd3), lambda si, bi: (bi, 0, si, 0))
                          for _, dil in DILATED_GROUPS),
        scratch_shapes=[pltpu.VMEM((d // V7X_LANES, ts, V7X_LANES), F32),
                        pltpu.VMEM((2, 2, ts * nj, V7X_LANES), jnp.uint32),
                        pltpu.SemaphoreType.DMA((2,))],
        compiler_params=_params(2),
        name="combine_qkv_proj",
    )(row_t, row_t, x1, plan, mod0, mod, g_mix, w_in, *_rope_tables(s, ts), ys)


def _attn_kernel(q_ref, kc_ref, kp_ref, vc_ref, vp_ref, o_ref, lse_ref):
    tq = q_ref.shape[2]
    sp = ATTN_SPAN
    nl = lse_ref.shape[3]
    first_tile = pl.program_id(2) == 0
    kext = jnp.concatenate([kp_ref[0, 0], kc_ref[0, 0]], axis=0)
    vext = jnp.concatenate([vp_ref[0, 0], vc_ref[0, 0]], axis=0)
    q_t = (q_ref[0, 0].astype(F32) * (HEAD_DIM ** -0.5)).T.astype(BF16)
    v_t = vext.astype(F32).T.astype(BF16)
    kj = lax.broadcasted_iota(jnp.int32, (2 * sp, sp), 0)
    qi = lax.broadcasted_iota(jnp.int32, (2 * sp, sp), 1)
    neg = jnp.float32(-jnp.inf)
    band = jnp.where(kj >= qi, jnp.where(kj <= qi + sp, 0.0, neg), neg)
    zeros_half = jnp.zeros((HEAD_DIM, sp), BF16)
    head_row = lax.broadcasted_iota(jnp.int32, (ATTN_HEADS, sp), 0)
    nsub = tq // sp
    biases = [jnp.where(first_tile & (kj < sp), neg, band)] + [band] * (nsub - 1)
    o_rows = [[] for _ in range(nsub)]
    lse_ts = [jnp.zeros((ATTN_HEADS, sp), F32) for _ in range(nsub)]
    items = [(i, h) for i in range(nsub) for h in range(ATTN_HEADS)]
    for g0 in range(0, len(items), ATTN_ITEM_GROUP):
        group = items[g0:g0 + ATTN_ITEM_GROUP]
        scores = []
        for i, h in group:
            pair = h // 2
            k_pair = kext[i * sp:(i + 2) * sp, pair * 2 * HEAD_DIM:(pair + 1) * 2 * HEAD_DIM]
            q_h = q_t[h * HEAD_DIM:(h + 1) * HEAD_DIM, i * sp:(i + 1) * sp]
            q_m = jnp.concatenate([q_h, zeros_half] if h % 2 == 0 else [zeros_half, q_h], axis=0)
            scores.append(jnp.dot(k_pair, q_m, preferred_element_type=F32) + biases[i])
        probs, dens = [], []
        for (i, h), sc in zip(group, scores):
            m = jnp.max(sc, axis=0, keepdims=True)
            p = jnp.exp(sc - m)
            den = jnp.sum(p, axis=0, keepdims=True)
            probs.append(p.astype(BF16))
            dens.append(den)
            lse_ts[i] = jnp.where(head_row == h, m + jnp.log(den), lse_ts[i])
        for (i, h), p, den in zip(group, probs, dens):
            v_h = v_t[h * HEAD_DIM:(h + 1) * HEAD_DIM, i * sp:(i + 2) * sp]
            o_rows[i].append(jnp.dot(v_h, p, preferred_element_type=F32) / den)
    out_cols = [jnp.concatenate(rows, axis=0) for rows in o_rows]
    lse_cols = [jnp.concatenate([t, jnp.zeros((nl - ATTN_HEADS, sp), F32)], axis=0) for t in lse_ts]
    o_ref[0, 0] = jnp.concatenate(out_cols, axis=1).T.astype(BF16)
    lse_ref[0, 0] = jnp.concatenate(lse_cols, axis=1).T


def _dilated_attention(qkv_g):
    b, dil, L, _ = qkv_g.shape
    tq = min(ATTN_Q_TILE, L)
    sub = tq // ATTN_SPAN
    wd = ATTN_OUT_DIM

    def cur(j):
        return pl.BlockSpec((1, 1, tq, wd), lambda bi, r, lb: (bi, r, lb, j))

    def prev(j):
        return pl.BlockSpec((1, 1, ATTN_SPAN, wd),
                            lambda bi, r, lb: (bi, r, jnp.maximum(lb * sub - 1, 0), j))

    return pl.pallas_call(
        _attn_kernel,
        out_shape=(jax.ShapeDtypeStruct((b, dil, L, wd), BF16),
                   jax.ShapeDtypeStruct((b, dil, L, V7X_LANES), F32)),
        grid=(b, dil, L // tq),
        in_specs=[cur(0), cur(1), prev(1), cur(2), prev(2)],
        out_specs=(pl.BlockSpec((1, 1, tq, wd), lambda bi, r, lb: (bi, r, lb, 0)),
                   pl.BlockSpec((1, 1, tq, V7X_LANES), lambda bi, r, lb: (bi, r, lb, 0))),
        compiler_params=_params(3),
        name=f"dilated_attention_{dil}",
    )(qkv_g, qkv_g, qkv_g, qkv_g, qkv_g)


def _attn_out_kernel(o0_ref, o1_ref, o2_ref, l0_ref, l1_ref, l2_ref, x_ref, mod_ref, wout_ref, e_ref,
                     g2_ref, wr_ref, br_ref, x1_ref, h2_ref, lg_ref, o_scr, l_scr):
    ts = x_ref.shape[1]
    nc = ATTN_OUT_DIM // V7X_LANES
    for g, (o_ref, l_ref) in enumerate(((o0_ref, l0_ref), (o1_ref, l1_ref), (o2_ref, l2_ref))):
        dil = DILATED_GROUPS[g][1]
        for r in range(dil):
            rows = pl.ds(r, ts // dil, stride=dil) if dil > 1 else slice(None)
            o_r = o_ref[0, r].astype(F32)
            for cc in range(nc):
                o_scr[g, cc, rows, :] = o_r[:, cc * V7X_LANES:(cc + 1) * V7X_LANES]
            l_scr[g, rows, :] = l_ref[0, r]
    l0, l1, l2 = l_scr[0], l_scr[1], l_scr[2]
    m = jnp.maximum(jnp.maximum(l0, l1), l2)
    es = (jnp.exp(l0 - m), jnp.exp(l1 - m), jnp.exp(l2 - m))
    den = es[0] + es[1] + es[2]
    merged = jnp.zeros((ts, ATTN_OUT_DIM), F32)
    for g in range(N_DIL_GROUPS):
        w = es[g] / den
        w_hi = w.astype(BF16)
        w_lo = (w - w_hi.astype(F32)).astype(BF16)
        wfull = (jnp.dot(w_hi, e_ref[...], preferred_element_type=F32)
                 + jnp.dot(w_lo, e_ref[...], preferred_element_type=F32))
        merged = merged + wfull * jnp.concatenate([o_scr[g, cc] for cc in range(nc)], axis=1)
    mod = mod_ref[0]
    merged = merged.astype(BF16)
    nr = MIXER_ROW_CHUNKS if ts % (8 * MIXER_ROW_CHUNKS) == 0 else 1
    hr = ts // nr
    for r in range(nr):
        y = jnp.dot(merged[r * hr:(r + 1) * hr], wout_ref[...], preferred_element_type=F32)
        x1 = x_ref[0, r * hr:(r + 1) * hr, :] + mod[2:3] * y
        x1_ref[0, r * hr:(r + 1) * hr, :] = x1
        _ffn_input_and_logits(x1, mod, g2_ref, wr_ref, br_ref, h2_ref, lg_ref, r * hr)


def _attn_out(outs, lses, x, mod, w_out, g_ffn, wr, br):
    b, s, d = x.shape
    ts = min(ROW_TILE, s)
    nst = s // ts
    nj = d // (2 * V7X_LANES)
    const = lambda bi, si: (0, 0)
    tok = lambda bi, si: (bi * nst + si, 0)
    stream = lambda bi, si: (bi, 0, si, 0)
    expand = (jnp.arange(V7X_LANES)[:, None] == jnp.arange(ATTN_OUT_DIM)[None, :] // HEAD_DIM).astype(BF16)
    return pl.pallas_call(
        _attn_out_kernel,
        out_shape=(jax.ShapeDtypeStruct((b, s, d), F32),
                   jax.ShapeDtypeStruct((b * s * nj, V7X_LANES), jnp.uint32),
                   jax.ShapeDtypeStruct((b * s, V7X_LANES), F32)),
        grid=(b, nst),
        in_specs=[pl.BlockSpec((1, dil, ts // dil, ATTN_OUT_DIM), stream) for _, dil in DILATED_GROUPS]
                 + [pl.BlockSpec((1, dil, ts // dil, V7X_LANES), stream) for _, dil in DILATED_GROUPS]
                 + [pl.BlockSpec((1, ts, d), lambda bi, si: (bi, si, 0)),
                    pl.BlockSpec((1, 6, d), lambda bi, si: (bi, 0, 0)),
                    pl.BlockSpec((ATTN_OUT_DIM, d), const),
                    pl.BlockSpec((V7X_LANES, ATTN_OUT_DIM), const),
                    pl.BlockSpec((1, d), const),
                    pl.BlockSpec((d, V7X_LANES), const),
                    pl.BlockSpec((1, V7X_LANES), const)],
        out_specs=(pl.BlockSpec((1, ts, d), lambda bi, si: (bi, si, 0)),
                   pl.BlockSpec((ts * nj, V7X_LANES), tok),
                   pl.BlockSpec((ts, V7X_LANES), tok)),
        scratch_shapes=[pltpu.VMEM((N_DIL_GROUPS, ATTN_OUT_DIM // V7X_LANES, ts, V7X_LANES), F32),
                        pltpu.VMEM((N_DIL_GROUPS, ts, V7X_LANES), F32)],
        compiler_params=_params(2),
        name="attn_out",
    )(*outs, *lses, x, mod, w_out, expand, g_ffn, wr, br)


def _router_params(w_grp, b_grp, w_exp, b_exp):
    d = w_grp.shape[0]
    w_e = jnp.transpose(w_exp, (1, 0, 2)).reshape(d, N_EXPERTS)
    pad1 = ROUTER_LANE_OFFSET - N_GROUPS
    pad2 = V7X_LANES - ROUTER_LANE_OFFSET - N_EXPERTS
    wr = jnp.concatenate([w_grp, jnp.zeros((d, pad1), F32), w_e, jnp.zeros((d, pad2), F32)], axis=1)
    br = jnp.concatenate([b_grp, jnp.zeros((pad1,), F32), b_exp.reshape(-1), jnp.zeros((pad2,), F32)])
    return wr.astype(BF16), br.reshape(1, V7X_LANES).astype(F32)


def kernel(x, c, norm_mix_g, norm_ffn_g, ada_w, ada_b, conv_in_w, conv_w, conv_out_w, attn_in_w,
           attn_out_w, router_grp_w, router_grp_b, router_exp_w, router_exp_b, exp_gate_w, exp_up_w,
           exp_down_w, final_norm_g):
    b, s, d = x.shape
    depth = ada_w.shape[0]
    assert depth == 2 and s % (DILATED_GROUPS[-1][1] * ATTN_SPAN) == 0 and d % (2 * V7X_LANES) == 0
    mod = _ada_modulation(c, ada_w, ada_b).reshape(depth, b, 6, d)
    g_fin = final_norm_g.reshape(1, d)

    wr, br = _router_params(router_grp_w[0], router_grp_b[0], router_exp_w[0], router_exp_b[0])
    x1, h2, logits = _conv_mixer(x, mod[0], norm_mix_g[0:1], conv_in_w[0].astype(BF16), conv_w[0],
                                 conv_out_w[0].astype(BF16), norm_ffn_g[0:1], wr, br)
    row_t, plan, ys = _moe_experts(h2, logits, exp_gate_w, exp_up_w, exp_down_w, layer=0)

    wr, br = _router_params(router_grp_w[1], router_grp_b[1], router_exp_w[1], router_exp_b[1])
    x2, *qkv = _combine_qkv_proj(row_t, x1, plan, ys, mod[0], mod[1], norm_mix_g[1:2],
                                 attn_in_w[0].astype(BF16))
    outs, lses = zip(*[_dilated_attention(qkv_g) for qkv_g in qkv])
    x3, h2, logits = _attn_out(outs, lses, x2, mod[1], attn_out_w[0].astype(BF16), norm_ffn_g[1:2],
                               wr, br)
    row_t, plan, ys = _moe_experts(h2, logits, exp_gate_w, exp_up_w, exp_down_w, layer=1)
    return _combine_final(row_t, x3, plan, mod[1], g_fin, ys)
```

```python
import functools

import jax
import jax.numpy as jnp
from jax import lax
from jax.experimental import pallas as pl
from jax.experimental.pallas import tpu as pltpu

CONV_WIDTH = 3
DILATED_GROUPS = ((128, 1), (512, 4), (2048, 16))
N_DIL_GROUPS = len(DILATED_GROUPS)
ATTN_HEADS = 8
HEAD_DIM = 64
ATTN_OUT_DIM = ATTN_HEADS * HEAD_DIM
ATTN_IN_DIM = N_DIL_GROUPS * 3 * ATTN_OUT_DIM
ROT_DIM = HEAD_DIM // 4
ROPE_THETA = 500000.0
N_GROUPS = 4
EXPERTS_PER_GROUP = 8
N_EXPERTS = N_GROUPS * EXPERTS_PER_GROUP
NORM_EPS = 1e-6
ATTN_SPAN = 128

V7X_LANES = 128
V7X_VMEM_LIMIT_BYTES = 56 * 1024 * 1024

ROW_TILE = 512
CONV_COL_BLOCK = 256
MIXER_ROW_CHUNKS = 2
ROUTE_TILE = 1024
MOE_BLOCK = 512
EXPERT_ROW_CHUNKS = 2
ATTN_Q_TILE = 512
ATTN_ITEM_GROUP = 16
ROUTER_LANE_OFFSET = 8
DMA_ROW_TILE = 1024

F32 = jnp.float32
BF16 = jnp.bfloat16


def _params(n_axes):
    return pltpu.CompilerParams(dimension_semantics=("arbitrary",) * n_axes,
                                vmem_limit_bytes=V7X_VMEM_LIMIT_BYTES)


def _norm_mod(x, g, shift, scale):
    ms = jnp.mean(x * x, axis=-1, keepdims=True)
    y = x * lax.rsqrt(ms + NORM_EPS) * g
    return y * (1.0 + scale) + shift


def _pack_bf16_pair(a, b):
    return pltpu.pack_elementwise([a, b], packed_dtype=BF16)


def _unpack_bf16_pair(w):
    a = pltpu.unpack_elementwise(w, index=0, packed_dtype=BF16, unpacked_dtype=F32)
    b = pltpu.unpack_elementwise(w, index=1, packed_dtype=BF16, unpacked_dtype=F32)
    return a, b


def _store_row_groups(ref, x, first=0):
    rows, d = x.shape
    nj = d // (2 * V7X_LANES)
    for j in range(nj):
        lo = x[:, j * V7X_LANES:(j + 1) * V7X_LANES]
        hi = x[:, d // 2 + j * V7X_LANES:d // 2 + (j + 1) * V7X_LANES]
        ref[pl.ds(first * nj + j, rows, stride=nj), :] = _pack_bf16_pair(lo, hi)


def _load_row_groups(ref, rows, nj, first=0):
    los, his = [], []
    for j in range(nj):
        lo, hi = _unpack_bf16_pair(ref[pl.ds(first * nj + j, rows, stride=nj), :])
        los.append(lo)
        his.append(hi)
    return los + his


def _slot_rows_block(ts):
    return (2 * ts // V7X_LANES, V7X_LANES)


def _ffn_input_and_logits(x1, mod, g2_ref, wr_ref, br_ref, h2_ref, lg_ref, first=0):
    rows = x1.shape[0]
    h2 = _norm_mod(x1, g2_ref[...], mod[3:4], mod[4:5])
    _store_row_groups(h2_ref, h2, first)
    lg_ref[first:first + rows, :] = (jnp.dot(h2.astype(BF16), wr_ref[...], preferred_element_type=F32)
                                     + br_ref[...])


def _ada_kernel(c_ref, w_ref, b_ref, o_ref):
    c = c_ref[...]
    ca = c * jax.nn.sigmoid(c)
    o_ref[0] = jnp.dot(ca, w_ref[0], preferred_element_type=F32,
                       precision=lax.Precision.HIGHEST) + b_ref[0]


def _ada_modulation(c, ada_w, ada_b):
    depth, d, n = ada_w.shape
    b = c.shape[0]
    tn = d
    return pl.pallas_call(
        _ada_kernel,
        out_shape=jax.ShapeDtypeStruct((depth, b, n), F32),
        grid=(depth, n // tn),
        in_specs=[pl.BlockSpec((b, d), lambda i, j: (0, 0)),
                  pl.BlockSpec((1, d, tn), lambda i, j: (i, 0, j)),
                  pl.BlockSpec((1, 1, tn), lambda i, j: (i, 0, j))],
        out_specs=pl.BlockSpec((1, b, tn), lambda i, j: (i, 0, j)),
        compiler_params=_params(2),
        name="ada_modulation",
    )(c, ada_w, ada_b.reshape(depth, 1, n))


def _conv_mixer_kernel(x_ref, mod_ref, g_ref, win_ref, cw_ref, wout_ref, g2_ref, wr_ref, br_ref,
                       x1_ref, h2_ref, lg_ref, carry_ref, v_ref, *, cb):
    ts, d = x_ref.shape[1], x_ref.shape[2]

    @pl.when(pl.program_id(1) == 0)
    def _():
        carry_ref[...] = jnp.zeros_like(carry_ref)

    mod = mod_ref[0]
    nr = MIXER_ROW_CHUNKS if ts % (8 * MIXER_ROW_CHUNKS) == 0 else 1
    hr = ts // nr
    xs = [x_ref[0, r * hr:(r + 1) * hr, :] for r in range(nr)]
    hs = [_norm_mod(x, g_ref[...], mod[0:1], mod[1:2]).astype(BF16) for x in xs]
    row = lax.broadcasted_iota(jnp.int32, (hr, cb), 0)
    tails = [carry_ref[:, j * cb:(j + 1) * cb] for j in range(d // cb)]
    for r in range(nr):
        for j in range(d // cb):
            lo = j * cb
            bj = jnp.dot(hs[r], win_ref[:, lo:lo + cb], preferred_element_type=F32)
            cj = jnp.dot(hs[r], win_ref[:, d + lo:d + lo + cb], preferred_element_type=F32)
            uj = jnp.dot(hs[r], win_ref[:, 2 * d + lo:2 * d + lo + cb], preferred_element_type=F32)
            z = cj * uj
            prev = tails[j]
            z1 = jnp.where(row == 0, prev[7:8], pltpu.roll(z, 1, 0))
            z2 = jnp.where(row == 0, prev[6:7], jnp.where(row == 1, prev[7:8], pltpu.roll(z, 2, 0)))
            zc = cw_ref[0:1, lo:lo + cb] * z2 + cw_ref[1:2, lo:lo + cb] * z1 + cw_ref[2:3, lo:lo + cb] * z
            tails[j] = z[hr - 8:hr]
            v_ref[r * hr:(r + 1) * hr, lo:lo + cb] = (bj * zc).astype(BF16)
    for j in range(d // cb):
        carry_ref[:, j * cb:(j + 1) * cb] = tails[j]
    for r in range(nr):
        y = jnp.dot(v_ref[r * hr:(r + 1) * hr, :], wout_ref[...], preferred_element_type=F32)
        x1 = xs[r] + mod[2:3] * y
        x1_ref[0, r * hr:(r + 1) * hr, :] = x1
        _ffn_input_and_logits(x1, mod, g2_ref, wr_ref, br_ref, h2_ref, lg_ref, r * hr)


def _conv_mixer(x, mod, g_mix, w_in, conv_w, w_out, g_ffn, wr, br):
    b, s, d = x.shape
    ts = min(ROW_TILE, s)
    cb = min(CONV_COL_BLOCK, d)
    nst = s // ts
    nj = d // (2 * V7X_LANES)
    const = lambda bi, si: (0, 0)
    return pl.pallas_call(
        functools.partial(_conv_mixer_kernel, cb=cb),
        out_shape=(jax.ShapeDtypeStruct((b, s, d), F32),
                   jax.ShapeDtypeStruct((b * s * nj, V7X_LANES), jnp.uint32),
                   jax.ShapeDtypeStruct((b * s, V7X_LANES), F32)),
        grid=(b, nst),
        in_specs=[pl.BlockSpec((1, ts, d), lambda bi, si: (bi, si, 0)),
                  pl.BlockSpec((1, 6, d), lambda bi, si: (bi, 0, 0)),
                  pl.BlockSpec((1, d), const),
                  pl.BlockSpec((d, 3 * d), const),
                  pl.BlockSpec((CONV_WIDTH, d), const),
                  pl.BlockSpec((d, d), const),
                  pl.BlockSpec((1, d), const),
                  pl.BlockSpec((d, V7X_LANES), const),
                  pl.BlockSpec((1, V7X_LANES), const)],
        out_specs=(pl.BlockSpec((1, ts, d), lambda bi, si: (bi, si, 0)),
                   pl.BlockSpec((ts * nj, V7X_LANES), lambda bi, si: (bi * nst + si, 0)),
                   pl.BlockSpec((ts, V7X_LANES), lambda bi, si: (bi * nst + si, 0))),
        scratch_shapes=[pltpu.VMEM((8, d), F32), pltpu.VMEM((ts, d), BF16)],
        compiler_params=_params(2),
        name="conv_mixer",
    )(x, mod, g_mix, w_in, conv_w, w_out, g_ffn, wr, br)


def _route_kernel(lg_ref, tri_ref, plan_ref, cnt_ref, run_ref, pst_ref, *, blk):
    ph, i = pl.program_id(0), pl.program_id(1)
    tr, lanes = lg_ref.shape
    epg = EXPERTS_PER_GROUP

    @pl.when((ph == 0) & (i == 0))
    def _():
        run_ref[...] = jnp.zeros_like(run_ref)

    lt = lg_ref[...].T
    row8 = lax.broadcasted_iota(jnp.int32, (epg, tr), 0).astype(F32)
    neg = jnp.float32(-jnp.inf)
    big = jnp.float32(epg)
    gmask = row8 < N_GROUPS
    gl = jnp.where(gmask, lt[0:epg], neg)
    mg = jnp.max(gl, axis=0, keepdims=True)
    gidx = jnp.min(jnp.where(gl == mg, row8, big), axis=0, keepdims=True)
    den_g = jnp.sum(jnp.where(gmask, jnp.exp(lt[0:epg] - mg), 0.0), axis=0, keepdims=True)
    grp_p = 1.0 / den_g
    sel = lt[ROUTER_LANE_OFFSET:ROUTER_LANE_OFFSET + epg]
    for g in range(1, N_GROUPS):
        sel = jnp.where(gidx == g, lt[ROUTER_LANE_OFFSET + g * epg:ROUTER_LANE_OFFSET + (g + 1) * epg], sel)
    l1 = jnp.max(sel, axis=0, keepdims=True)
    i1 = jnp.min(jnp.where(sel == l1, row8, big), axis=0, keepdims=True)
    sel2 = jnp.where(row8 == i1, neg, sel)
    l2 = jnp.max(sel2, axis=0, keepdims=True)
    i2 = jnp.min(jnp.where(sel2 == l2, row8, big), axis=0, keepdims=True)
    r = jnp.exp(l2 - l1)
    w1 = grp_p / (1.0 + r)
    w2 = grp_p * r / (1.0 + r)
    e1 = gidx * epg + i1
    e2 = gidx * epg + i2
    row_e = lax.broadcasted_iota(jnp.int32, (N_EXPERTS, tr), 0).astype(F32)
    oh1 = (row_e == e1).astype(F32)
    oh2 = (row_e == e2).astype(F32)
    tot1 = jnp.sum(oh1, axis=1, keepdims=True)
    tot2 = jnp.sum(oh2, axis=1, keepdims=True)
    run = run_ref[:, 0:1]

    @pl.when(ph == 0)
    def _():
        new_run = run + tot1 + tot2
        run_ref[...] = jnp.broadcast_to(new_run, run_ref.shape)

        @pl.when(i == pl.num_programs(1) - 1)
        def _():
            cnt = jnp.broadcast_to(new_run, run_ref.shape)
            padded = jnp.floor((cnt + (blk - 1)) / blk) * blk
            rid = lax.broadcasted_iota(jnp.int32, cnt.shape, 0)
            csum = padded
            k = 1
            while k < N_EXPERTS:
                csum = csum + jnp.where(rid >= k, pltpu.roll(csum, k, 0), 0.0)
                k *= 2
            pst_ref[...] = csum - padded
            run_ref[...] = jnp.zeros_like(run_ref)
            lid = lax.broadcasted_iota(jnp.int32, cnt.shape, 1)
            cnt_ref[...] = jnp.where(lid == 0, cnt, jnp.where(lid == 1, csum, jnp.where(lid == 2, padded, 0.0)))

    @pl.when(ph == 1)
    def _():
        base = pst_ref[:, 0:1] + run
        oh = jnp.concatenate([oh1, oh2], axis=0).astype(BF16)
        off = jnp.concatenate([base, base + tot1], axis=0)
        pos1, pos2 = [], []
        for c in range(tr // lanes):
            ohc = oh[:, c * lanes:(c + 1) * lanes]
            before = jnp.dot(ohc, tri_ref[...], preferred_element_type=F32) + off
            ohf = ohc.astype(F32)
            pos1.append(jnp.sum(ohf[:N_EXPERTS] * before[:N_EXPERTS], axis=0, keepdims=True))
            pos2.append(jnp.sum(ohf[N_EXPERTS:] * before[N_EXPERTS:], axis=0, keepdims=True))
            off = off + jnp.sum(ohf, axis=1, keepdims=True)
        run_ref[...] = jnp.broadcast_to(run + tot1 + tot2, run_ref.shape)
        pos1 = jnp.concatenate(pos1, axis=1)
        pos2 = jnp.concatenate(pos2, axis=1)
        head = jnp.where(row8 == 0, pos1, jnp.where(row8 == 1, pos2,
                         jnp.where(row8 == 2, w1, jnp.where(row8 == 3, w2, 0.0))))
        plan_t = jnp.concatenate([head, jnp.zeros((lanes - epg, tr), F32)], axis=0)
        plan_ref[...] = plan_t.T


def _route(logits):
    t, lanes = logits.shape
    tr = min(ROUTE_TILE, t)
    nt = t // tr
    tri = (jnp.arange(lanes)[:, None] < jnp.arange(lanes)[None, :]).astype(BF16)
    return pl.pallas_call(
        functools.partial(_route_kernel, blk=MOE_BLOCK),
        out_shape=(jax.ShapeDtypeStruct((t, lanes), F32),
                   jax.ShapeDtypeStruct((N_EXPERTS, lanes), F32)),
        grid=(2, nt),
        in_specs=[pl.BlockSpec((tr, lanes), lambda ph, i: (i, 0)),
                  pl.BlockSpec((lanes, lanes), lambda ph, i: (0, 0))],
        out_specs=(pl.BlockSpec((tr, lanes), lambda ph, i: (i * ph, 0)),
                   pl.BlockSpec((N_EXPERTS, lanes), lambda ph, i: (0, 0))),
        scratch_shapes=[pltpu.VMEM((N_EXPERTS, lanes), F32), pltpu.VMEM((N_EXPERTS, lanes), F32)],
        compiler_params=_params(2),
        name="route",
    )(logits, tri)


def _dispatch_kernel(last_ref, row_ref, h2_ref, xs_hbm, zero_buf, sem, zero_sem, *, nj):
    ts = h2_ref.shape[0] // nj
    blk_rows = zero_buf.shape[0]

    @pl.when(pl.program_id(0) == 0)
    def _():
        z = jnp.zeros(zero_buf.shape, F32)
        zero_buf[...] = _pack_bf16_pair(z, z)

        def blank(e):
            dst = xs_hbm.at[pl.ds(pl.multiple_of(last_ref[e], nj), blk_rows)]
            return pltpu.make_async_copy(zero_buf, dst, zero_sem)

        def zstart(e, carry):
            @pl.when(last_ref[e] >= 0)
            def _():
                blank(e).start()
            return carry

        def zwait(e, carry):
            @pl.when(last_ref[e] >= 0)
            def _():
                blank(e).wait()
            return carry

        def tail(bi):
            dst = xs_hbm.at[pl.ds(pl.multiple_of(bi * blk_rows, blk_rows), blk_rows)]
            return pltpu.make_async_copy(zero_buf, dst, zero_sem)

        def tstart(bi, carry):
            tail(bi).start()
            return carry

        def twait(bi, carry):
            tail(bi).wait()
            return carry

        n_blocks = xs_hbm.shape[0] // blk_rows
        lax.fori_loop(0, N_EXPERTS, zstart, 0)
        lax.fori_loop(last_ref[N_EXPERTS], n_blocks, tstart, 0)
        lax.fori_loop(0, N_EXPERTS, zwait, 0)
        lax.fori_loop(last_ref[N_EXPERTS], n_blocks, twait, 0)

    chunk = row_ref.shape[1]

    def start(c, carry):
        t0 = pl.multiple_of(c * chunk, chunk)
        for u in range(chunk):
            src = h2_ref.at[pl.ds(pl.multiple_of((t0 + u) * nj, nj), nj)]
            for k in range(2):
                pltpu.make_async_copy(src, xs_hbm.at[pl.ds(pl.multiple_of(row_ref[2 * c + k, u], nj), nj)],
                                      sem).start(priority=k)
        return carry

    lax.fori_loop(0, ts // chunk, start, 0)
    for _ in range(2):
        pltpu.make_async_copy(h2_ref, xs_hbm.at[pl.ds(0, ts * nj)], sem).wait()


def _dispatch(last_row, row_t, h2, n_rows, nj):
    t = h2.shape[0] // nj
    ts = min(DMA_ROW_TILE, t)
    grid_spec = pltpu.PrefetchScalarGridSpec(
        num_scalar_prefetch=1,
        grid=(t // ts,),
        in_specs=[pl.BlockSpec(_slot_rows_block(ts), lambda i, last: (i, 0), memory_space=pltpu.SMEM),
                  pl.BlockSpec((ts * nj, V7X_LANES), lambda i, last: (i, 0))],
        out_specs=pl.BlockSpec(memory_space=pl.ANY),
        scratch_shapes=[pltpu.VMEM((MOE_BLOCK * nj, V7X_LANES), h2.dtype),
                        pltpu.SemaphoreType.DMA(()), pltpu.SemaphoreType.DMA(())])
    return pl.pallas_call(
        functools.partial(_dispatch_kernel, nj=nj),
        out_shape=jax.ShapeDtypeStruct((n_rows * nj, V7X_LANES), h2.dtype),
        grid_spec=grid_spec,
        compiler_params=_params(1),
        name="dispatch",
    )(last_row, row_t, h2)


def _experts_kernel(be_ref, nact_ref, xs_ref, wg_ref, wu_ref, wd_ref, ys_ref, wgb, wub, wdb, *, nj):
    i = pl.program_id(0)
    blk = xs_ref.shape[0] // nj

    @pl.when(i < nact_ref[0])
    def _():
        prev = be_ref[jnp.maximum(i - 1, 0)]

        @pl.when((i == 0) | (be_ref[i] != prev))
        def _():
            wgb[...] = wg_ref[0, 0].astype(BF16)
            wub[...] = wu_ref[0, 0].astype(BF16)
            wdb[...] = wd_ref[0, 0].astype(BF16)

        nchunk = EXPERT_ROW_CHUNKS if blk % (8 * EXPERT_ROW_CHUNKS) == 0 else 1
        rows = blk // nchunk
        xs = [jnp.concatenate([p.astype(BF16) for p in _load_row_groups(xs_ref, rows, nj, c * rows)], axis=1)
              for c in range(nchunk)]
        gs = [jnp.dot(x, wgb[...], preferred_element_type=F32) for x in xs]
        us = [jnp.dot(x, wub[...], preferred_element_type=F32) for x in xs]
        acts = [(g * jax.nn.sigmoid(g) * u).astype(BF16) for g, u in zip(gs, us)]
        ys = [jnp.dot(a, wdb[...], preferred_element_type=F32) for a in acts]
        for c, y in enumerate(ys):
            _store_row_groups(ys_ref, y, c * rows)

    @pl.when(i >= nact_ref[0])
    def _():
        z = jnp.zeros(ys_ref.shape, F32)
        ys_ref[...] = _pack_bf16_pair(z, z)


def _experts(block_e, n_active, xs, w_gate, w_up, w_down, layer):
    _, _, d, de = w_gate.shape
    nj = d // (2 * V7X_LANES)
    n_blocks = xs.shape[0] // (MOE_BLOCK * nj)
    grid_spec = pltpu.PrefetchScalarGridSpec(
        num_scalar_prefetch=2,
        grid=(n_blocks,),
        in_specs=[pl.BlockSpec((MOE_BLOCK * nj, V7X_LANES), lambda i, be, na: (i, 0)),
                  pl.BlockSpec((1, 1, d, de), lambda i, be, na: (layer, be[i], 0, 0)),
                  pl.BlockSpec((1, 1, d, de), lambda i, be, na: (layer, be[i], 0, 0)),
                  pl.BlockSpec((1, 1, de, d), lambda i, be, na: (layer, be[i], 0, 0))],
        out_specs=pl.BlockSpec((MOE_BLOCK * nj, V7X_LANES), lambda i, be, na: (i, 0)),
        scratch_shapes=[pltpu.VMEM((d, de), BF16), pltpu.VMEM((d, de), BF16),
                        pltpu.VMEM((de, d), BF16)])
    return pl.pallas_call(
        functools.partial(_experts_kernel, nj=nj),
        out_shape=jax.ShapeDtypeStruct(xs.shape, jnp.uint32),
        grid_spec=grid_spec,
        compiler_params=_params(1),
        name="experts",
    )(block_e, n_active, xs, w_gate, w_up, w_down)


def _start_row_gathers(row_ref, ys_hbm, buf, sem, ts, nj):
    chunk = row_ref.shape[1]

    def start(c, carry):
        t0 = pl.multiple_of(c * chunk, chunk)
        for u in range(chunk):
            dst = pl.ds(pl.multiple_of((t0 + u) * nj, nj), nj)
            for k in range(2):
                pltpu.make_async_copy(ys_hbm.at[pl.ds(pl.multiple_of(row_ref[2 * c + k, u], nj), nj)],
                                      buf.at[k, dst], sem).start(priority=k)
        return carry

    lax.fori_loop(0, ts // chunk, start, 0)


def _wait_row_gathers(ys_hbm, buf, sem, ts, nj):
    for k in range(2):
        pltpu.make_async_copy(ys_hbm.at[pl.ds(0, ts * nj)], buf.at[k], sem).wait()


def _weighted_expert_rows(buf, plan_ref, ts, nj):
    w1 = plan_ref[:, 2:3]
    w2 = plan_ref[:, 3:4]
    p1 = _load_row_groups(buf.at[0], ts, nj)
    p2 = _load_row_groups(buf.at[1], ts, nj)
    return jnp.concatenate([w1 * a + w2 * b for a, b in zip(p1, p2)], axis=1)


def _gathered_expert_rows(step, n_steps, rcur_ref, rnext_ref, plan_ref, ys_hbm, buf, sems, ts, nj):
    slot = lax.rem(step, 2)

    @pl.when(step == 0)
    def _():
        _start_row_gathers(rcur_ref, ys_hbm, buf.at[0], sems.at[0], ts, nj)

    @pl.when(step + 1 < n_steps)
    def _():
        _start_row_gathers(rnext_ref, ys_hbm, buf.at[1 - slot], sems.at[1 - slot], ts, nj)

    _wait_row_gathers(ys_hbm, buf.at[slot], sems.at[slot], ts, nj)
    return _weighted_expert_rows(buf.at[slot], plan_ref, ts, nj)


def _combine_kernel(rcur_ref, rnext_ref, x_ref, plan_ref, mod_ref, gfin_ref, ys_hbm, out_ref, buf, sems, *, nj):
    ts = x_ref.shape[1]
    moe = _gathered_expert_rows(pl.program_id(0), pl.num_programs(0), rcur_ref, rnext_ref, plan_ref,
                                ys_hbm, buf, sems, ts, nj)
    x2 = x_ref[0] + mod_ref[0][5:6] * moe
    ms = jnp.mean(x2 * x2, axis=-1, keepdims=True)
    out_ref[0] = x2 * lax.rsqrt(ms + NORM_EPS) * gfin_ref[...]


def _combine_final(row_t, x1, plan, mod, g_final, ys):
    b, s, d = x1.shape
    ts = min(DMA_ROW_TILE, s)
    nst = s // ts
    n_tiles = b * nst
    nj = d // (2 * V7X_LANES)
    return pl.pallas_call(
        functools.partial(_combine_kernel, nj=nj),
        out_shape=jax.ShapeDtypeStruct((b, s, d), F32),
        grid=(n_tiles,),
        in_specs=[pl.BlockSpec(_slot_rows_block(ts), lambda i: (i, 0), memory_space=pltpu.SMEM),
                  pl.BlockSpec(_slot_rows_block(ts), lambda i: (jnp.minimum(i + 1, n_tiles - 1), 0),
                               memory_space=pltpu.SMEM),
                  pl.BlockSpec((1, ts, d), lambda i: (i // nst, i % nst, 0)),
                  pl.BlockSpec((ts, plan.shape[1]), lambda i: (i, 0)),
                  pl.BlockSpec((1, 6, d), lambda i: (i // nst, 0, 0)),
                  pl.BlockSpec((1, d), lambda i: (0, 0)),
                  pl.BlockSpec(memory_space=pl.ANY)],
        out_specs=pl.BlockSpec((1, ts, d), lambda i: (i // nst, i % nst, 0)),
        scratch_shapes=[pltpu.VMEM((2, 2, ts * nj, V7X_LANES), jnp.uint32),
                        pltpu.SemaphoreType.DMA((2,))],
        compiler_params=_params(1),
        name="combine_final",
    )(row_t, row_t, x1, plan, mod, g_final, ys)


def _moe_experts(h2, logits, w_gate, w_up, w_down, *, layer):
    t = logits.shape[0]
    nj = h2.shape[0] // t
    plan, cnt = _route(logits)
    n_blocks = -(-(t * 2) // MOE_BLOCK) + N_EXPERTS
    seg_end, seg_rows = cnt[:, 1], cnt[:, 2]
    pends = (seg_end * (1.0 / MOE_BLOCK)).astype(jnp.int32)
    block_e = jnp.minimum(jnp.sum(pends[None, :] <= jnp.arange(n_blocks)[:, None], axis=1),
                          N_EXPERTS - 1).astype(jnp.int32)
    n_active = pends[-1:]
    last_row = jnp.where(seg_rows > 0, (seg_end - MOE_BLOCK) * nj, -1.0).astype(jnp.int32)
    last_row = jnp.concatenate([last_row, n_active])
    row_t = (plan[:, 0:2] * nj).astype(jnp.int32).reshape(t // V7X_LANES, V7X_LANES, 2)
    row_t = jnp.transpose(row_t, (0, 2, 1)).reshape(2 * t // V7X_LANES, V7X_LANES)
    xs = _dispatch(last_row, row_t, h2, n_blocks * MOE_BLOCK, nj)
    ys = _experts(block_e, n_active, xs, w_gate, w_up, w_down, layer)
    return row_t, plan, ys


def _qkv_kernel(rcur_ref, rnext_ref, x_ref, plan_ref, mod0_ref, mod_ref, g_ref, w_ref, *refs, nj):
    tabs, ys_hbm, x2_ref, (o0_ref, o1_ref, o2_ref) = refs[:9], refs[9], refs[10], refs[11:14]
    scr, buf, sems = refs[14:17]
    ts, d = x_ref.shape[1], x_ref.shape[2]
    step = pl.program_id(0) * pl.num_programs(1) + pl.program_id(1)
    n_steps = pl.num_programs(0) * pl.num_programs(1)
    moe = _gathered_expert_rows(step, n_steps, rcur_ref, rnext_ref, plan_ref, ys_hbm, buf, sems, ts, nj)
    x2 = x_ref[0] + mod0_ref[0][5:6] * moe
    x2_ref[0] = x2
    mod = mod_ref[0]
    hn = _norm_mod(x2, g_ref[...], mod[0:1], mod[1:2])
    wd = ATTN_OUT_DIM
    rep = wd // V7X_LANES
    nc = d // V7X_LANES
    for g, o_ref in enumerate((o0_ref, o1_ref, o2_ref)):
        dil = DILATED_GROUPS[g][1]
        n = ts // dil
        if dil == 1:
            h = hn.astype(BF16)
        else:
            if g == 1:
                for cc in range(nc):
                    scr[cc] = hn[:, cc * V7X_LANES:(cc + 1) * V7X_LANES]
            h = jnp.concatenate(
                [jnp.concatenate([scr[cc, pl.ds(r, n, stride=dil), :] for cc in range(nc)], axis=1)
                 for r in range(dil)], axis=0).astype(BF16)
        cos, sa, sb = (jnp.tile(t[...], (1, rep)) for t in tabs[3 * g:3 * g + 3])
        for j in range(3):
            slab = g * 3 + j
            acc = jnp.dot(h, w_ref[:, slab * wd:(slab + 1) * wd], preferred_element_type=F32)
            if j != 2:
                acc = (acc * cos + pltpu.roll(acc, ROT_DIM // 2, 1) * sa
                       + pltpu.roll(acc, wd - ROT_DIM // 2, 1) * sb)
            acc = acc.astype(BF16)
            for r in range(dil):
                o_ref[0, r, :, j * wd:(j + 1) * wd] = acc[r * n:(r + 1) * n]


def _rope_tables(s, ts):
    half = ROT_DIM // 2
    pos = jnp.arange(s, dtype=F32)
    inv_freq = jnp.power(jnp.float32(ROPE_THETA), -jnp.arange(0, ROT_DIM, 2, dtype=F32) / ROT_DIM)
    ang = pos[:, None] * inv_freq[None, :]
    cos, sin = jnp.cos(ang), jnp.sin(ang)
    m = jnp.arange(V7X_LANES) % HEAD_DIM
    cos_l = jnp.where(m[None, :] < ROT_DIM, cos[:, m % half], 1.0)
    sa_l = jnp.where((m[None, :] >= half) & (m[None, :] < ROT_DIM), sin[:, m % half], 0.0)
    sb_l = jnp.where(m[None, :] < half, -sin[:, m % half], 0.0)
    tabs = []
    for _, dil in DILATED_GROUPS:
        for t in (cos_l, sa_l, sb_l):
            t = t.astype(F32).reshape(s // ts, ts // dil, dil, V7X_LANES)
            tabs.append(jnp.transpose(t, (0, 2, 1, 3)).reshape(s, V7X_LANES))
    return tabs


def _combine_qkv_proj(row_t, x1, plan, ys, mod0, mod, g_mix, w_in):
    b, s, d = x1.shape
    ts = min(ROW_TILE, s)
    nst = s // ts
    nj = d // (2 * V7X_LANES)
    const = lambda si, bi: (0, 0)
    tab = pl.BlockSpec((ts, V7X_LANES), lambda si, bi: (si, 0))
    wd3 = 3 * ATTN_OUT_DIM

    def next_tile(si, bi):
        nb = lax.rem(bi + 1, b)
        ns = jnp.minimum(si + lax.div(bi + 1, b), nst - 1)
        return (nb * nst + ns, 0)

    return pl.pallas_call(
        functools.partial(_qkv_kernel, nj=nj),
        out_shape=(jax.ShapeDtypeStruct((b, s, d), F32),)
                  + tuple(jax.ShapeDtypeStruct((b, dil, s // dil, wd3), BF16) for _, dil in DILATED_GROUPS),
        grid=(nst, b),
        in_specs=[pl.BlockSpec(_slot_rows_block(ts), lambda si, bi: (bi * nst + si, 0),
                               memory_space=pltpu.SMEM),
                  pl.BlockSpec(_slot_rows_block(ts), next_tile, memory_space=pltpu.SMEM),
                  pl.BlockSpec((1, ts, d), lambda si, bi: (bi, si, 0)),
                  pl.BlockSpec((ts, plan.shape[1]), lambda si, bi: (bi * nst + si, 0)),
                  pl.BlockSpec((1, 6, d), lambda si, bi: (bi, 0, 0)),
                  pl.BlockSpec((1, 6, d), lambda si, bi: (bi, 0, 0)),
                  pl.BlockSpec((1, d), const),
                  pl.BlockSpec((d, ATTN_IN_DIM), const)] + [tab] * 9
                 + [pl.BlockSpec(memory_space=pl.ANY)],
        out_specs=(pl.BlockSpec((1, ts, d), lambda si, bi: (bi, si, 0)),)
                  + tuple(pl.BlockSpec((1, dil, ts // dil, wd3), lambda si, bi: (bi, 0, si, 0))
                          for _, dil in DILATED_GROUPS),
        scratch_shapes=[pltpu.VMEM((d // V7X_LANES, ts, V7X_LANES), F32),
                        pltpu.VMEM((2, 2, ts * nj, V7X_LANES), jnp.uint32),
                        pltpu.SemaphoreType.DMA((2,))],
        compiler_params=_params(2),
        name="combine_qkv_proj",
    )(row_t, row_t, x1, plan, mod0, mod, g_mix, w_in, *_rope_tables(s, ts), ys)


def _attn_kernel(q_ref, kc_ref, kp_ref, vc_ref, vp_ref, o_ref, lse_ref):
    first_tile = pl.program_id(2) == 0
    for rr in range(q_ref.shape[1]):
        o, lse = _attend_stream(q_ref[0, rr], kc_ref[0, rr], kp_ref[0, rr], vc_ref[0, rr], vp_ref[0, rr],
                                first_tile, lse_ref.shape[3])
        o_ref[0, rr] = o
        lse_ref[0, rr] = lse


def _attend_stream(q, kc, kp, vc, vp, first_tile, nl):
    tq = q.shape[0]
    sp = ATTN_SPAN
    kext = jnp.concatenate([kp, kc], axis=0)
    vext = jnp.concatenate([vp, vc], axis=0)
    q_t = (q.astype(F32) * (HEAD_DIM ** -0.5)).T.astype(BF16)
    v_t = vext.astype(F32).T.astype(BF16)
    kj = lax.broadcasted_iota(jnp.int32, (2 * sp, sp), 0)
    qi = lax.broadcasted_iota(jnp.int32, (2 * sp, sp), 1)
    neg = jnp.float32(-jnp.inf)
    band = jnp.where(kj >= qi, jnp.where(kj <= qi + sp, 0.0, neg), neg)
    zeros_half = jnp.zeros((HEAD_DIM, sp), BF16)
    head_row = lax.broadcasted_iota(jnp.int32, (ATTN_HEADS, sp), 0)
    nsub = tq // sp
    biases = [jnp.where(first_tile & (kj < sp), neg, band)] + [band] * (nsub - 1)
    o_rows = [[] for _ in range(nsub)]
    lse_ts = [jnp.zeros((ATTN_HEADS, sp), F32) for _ in range(nsub)]
    items = [(i, h) for i in range(nsub) for h in range(ATTN_HEADS)]
    for g0 in range(0, len(items), ATTN_ITEM_GROUP):
        group = items[g0:g0 + ATTN_ITEM_GROUP]
        scores = []
        for i, h in group:
            pair = h // 2
            k_pair = kext[i * sp:(i + 2) * sp, pair * 2 * HEAD_DIM:(pair + 1) * 2 * HEAD_DIM]
            q_h = q_t[h * HEAD_DIM:(h + 1) * HEAD_DIM, i * sp:(i + 1) * sp]
            q_m = jnp.concatenate([q_h, zeros_half] if h % 2 == 0 else [zeros_half, q_h], axis=0)
            scores.append(jnp.dot(k_pair, q_m, preferred_element_type=F32) + biases[i])
        probs, dens = [], []
        for (i, h), sc in zip(group, scores):
            m = jnp.max(sc, axis=0, keepdims=True)
            p = jnp.exp(sc - m)
            den = jnp.sum(p, axis=0, keepdims=True)
            probs.append(p.astype(BF16))
            dens.append(den)
            lse_ts[i] = jnp.where(head_row == h, m + jnp.log(den), lse_ts[i])
        for (i, h), p, den in zip(group, probs, dens):
            v_h = v_t[h * HEAD_DIM:(h + 1) * HEAD_DIM, i * sp:(i + 2) * sp]
            o_rows[i].append(jnp.dot(v_h, p, preferred_element_type=F32) / den)
    out_cols = [jnp.concatenate(rows, axis=0) for rows in o_rows]
    lse_cols = [jnp.concatenate([t, jnp.zeros((nl - ATTN_HEADS, sp), F32)], axis=0) for t in lse_ts]
    return jnp.concatenate(out_cols, axis=1).T.astype(BF16), jnp.concatenate(lse_cols, axis=1).T


def _dilated_attention(qkv_g):
    b, dil, L, _ = qkv_g.shape
    tq = min(ATTN_Q_TILE, L)
    sub = tq // ATTN_SPAN
    wd = ATTN_OUT_DIM
    nres = ATTN_Q_TILE // tq if dil % (ATTN_Q_TILE // tq) == 0 else 1

    def cur(j):
        return pl.BlockSpec((1, nres, tq, wd), lambda bi, r, lb: (bi, r, lb, j))

    def prev(j):
        return pl.BlockSpec((1, nres, ATTN_SPAN, wd),
                            lambda bi, r, lb: (bi, r, jnp.maximum(lb * sub - 1, 0), j))

    return pl.pallas_call(
        _attn_kernel,
        out_shape=(jax.ShapeDtypeStruct((b, dil, L, wd), BF16),
                   jax.ShapeDtypeStruct((b, dil, L, V7X_LANES), F32)),
        grid=(b, dil // nres, L // tq),
        in_specs=[cur(0), cur(1), prev(1), cur(2), prev(2)],
        out_specs=(pl.BlockSpec((1, nres, tq, wd), lambda bi, r, lb: (bi, r, lb, 0)),
                   pl.BlockSpec((1, nres, tq, V7X_LANES), lambda bi, r, lb: (bi, r, lb, 0))),
        compiler_params=_params(3),
        name=f"dilated_attention_{dil}",
    )(qkv_g, qkv_g, qkv_g, qkv_g, qkv_g)


def _attn_out_kernel(o0_ref, o1_ref, o2_ref, l0_ref, l1_ref, l2_ref, x_ref, mod_ref, wout_ref, e_ref,
                     g2_ref, wr_ref, br_ref, x1_ref, h2_ref, lg_ref, o_scr, l_scr):
    ts = x_ref.shape[1]
    nc = ATTN_OUT_DIM // V7X_LANES
    for g, (o_ref, l_ref) in enumerate(((o0_ref, l0_ref), (o1_ref, l1_ref), (o2_ref, l2_ref))):
        dil = DILATED_GROUPS[g][1]
        for r in range(dil):
            rows = pl.ds(r, ts // dil, stride=dil) if dil > 1 else slice(None)
            o_r = o_ref[0, r].astype(F32)
            for cc in range(nc):
                o_scr[g, cc, rows, :] = o_r[:, cc * V7X_LANES:(cc + 1) * V7X_LANES]
            l_scr[g, rows, :] = l_ref[0, r]
    l0, l1, l2 = l_scr[0], l_scr[1], l_scr[2]
    m = jnp.maximum(jnp.maximum(l0, l1), l2)
    es = (jnp.exp(l0 - m), jnp.exp(l1 - m), jnp.exp(l2 - m))
    den = es[0] + es[1] + es[2]
    merged = jnp.zeros((ts, ATTN_OUT_DIM), F32)
    for g in range(N_DIL_GROUPS):
        w = es[g] / den
        w_hi = w.astype(BF16)
        w_lo = (w - w_hi.astype(F32)).astype(BF16)
        wfull = (jnp.dot(w_hi, e_ref[...], preferred_element_type=F32)
                 + jnp.dot(w_lo, e_ref[...], preferred_element_type=F32))
        merged = merged + wfull * jnp.concatenate([o_scr[g, cc] for cc in range(nc)], axis=1)
    mod = mod_ref[0]
    merged = merged.astype(BF16)
    nr = MIXER_ROW_CHUNKS if ts % (8 * MIXER_ROW_CHUNKS) == 0 else 1
    hr = ts // nr
    for r in range(nr):
        y = jnp.dot(merged[r * hr:(r + 1) * hr], wout_ref[...], preferred_element_type=F32)
        x1 = x_ref[0, r * hr:(r + 1) * hr, :] + mod[2:3] * y
        x1_ref[0, r * hr:(r + 1) * hr, :] = x1
        _ffn_input_and_logits(x1, mod, g2_ref, wr_ref, br_ref, h2_ref, lg_ref, r * hr)


def _attn_out(outs, lses, x, mod, w_out, g_ffn, wr, br):
    b, s, d = x.shape
    ts = min(ROW_TILE, s)
    nst = s // ts
    nj = d // (2 * V7X_LANES)
    const = lambda bi, si: (0, 0)
    tok = lambda bi, si: (bi * nst + si, 0)
    stream = lambda bi, si: (bi, 0, si, 0)
    expand = (jnp.arange(V7X_LANES)[:, None] == jnp.arange(ATTN_OUT_DIM)[None, :] // HEAD_DIM).astype(BF16)
    return pl.pallas_call(
        _attn_out_kernel,
        out_shape=(jax.ShapeDtypeStruct((b, s, d), F32),
                   jax.ShapeDtypeStruct((b * s * nj, V7X_LANES), jnp.uint32),
                   jax.ShapeDtypeStruct((b * s, V7X_LANES), F32)),
        grid=(b, nst),
        in_specs=[pl.BlockSpec((1, dil, ts // dil, ATTN_OUT_DIM), stream) for _, dil in DILATED_GROUPS]
                 + [pl.BlockSpec((1, dil, ts // dil, V7X_LANES), stream) for _, dil in DILATED_GROUPS]
                 + [pl.BlockSpec((1, ts, d), lambda bi, si: (bi, si, 0)),
                    pl.BlockSpec((1, 6, d), lambda bi, si: (bi, 0, 0)),
                    pl.BlockSpec((ATTN_OUT_DIM, d), const),
                    pl.BlockSpec((V7X_LANES, ATTN_OUT_DIM), const),
                    pl.BlockSpec((1, d), const),
                    pl.BlockSpec((d, V7X_LANES), const),
                    pl.BlockSpec((1, V7X_LANES), const)],
        out_specs=(pl.BlockSpec((1, ts, d), lambda bi, si: (bi, si, 0)),
                   pl.BlockSpec((ts * nj, V7X_LANES), tok),
                   pl.BlockSpec((ts, V7X_LANES), tok)),
        scratch_shapes=[pltpu.VMEM((N_DIL_GROUPS, ATTN_OUT_DIM // V7X_LANES, ts, V7X_LANES), F32),
                        pltpu.VMEM((N_DIL_GROUPS, ts, V7X_LANES), F32)],
        compiler_params=_params(2),
        name="attn_out",
    )(*outs, *lses, x, mod, w_out, expand, g_ffn, wr, br)


def _router_params(w_grp, b_grp, w_exp, b_exp):
    d = w_grp.shape[0]
    w_e = jnp.transpose(w_exp, (1, 0, 2)).reshape(d, N_EXPERTS)
    pad1 = ROUTER_LANE_OFFSET - N_GROUPS
    pad2 = V7X_LANES - ROUTER_LANE_OFFSET - N_EXPERTS
    wr = jnp.concatenate([w_grp, jnp.zeros((d, pad1), F32), w_e, jnp.zeros((d, pad2), F32)], axis=1)
    br = jnp.concatenate([b_grp, jnp.zeros((pad1,), F32), b_exp.reshape(-1), jnp.zeros((pad2,), F32)])
    return wr.astype(BF16), br.reshape(1, V7X_LANES).astype(F32)


def kernel(x, c, norm_mix_g, norm_ffn_g, ada_w, ada_b, conv_in_w, conv_w, conv_out_w, attn_in_w,
           attn_out_w, router_grp_w, router_grp_b, router_exp_w, router_exp_b, exp_gate_w, exp_up_w,
           exp_down_w, final_norm_g):
    b, s, d = x.shape
    depth = ada_w.shape[0]
    assert depth == 2 and s % (DILATED_GROUPS[-1][1] * ATTN_SPAN) == 0 and d % (2 * V7X_LANES) == 0
    mod = _ada_modulation(c, ada_w, ada_b).reshape(depth, b, 6, d)
    g_fin = final_norm_g.reshape(1, d)

    wr, br = _router_params(router_grp_w[0], router_grp_b[0], router_exp_w[0], router_exp_b[0])
    x1, h2, logits = _conv_mixer(x, mod[0], norm_mix_g[0:1], conv_in_w[0].astype(BF16), conv_w[0],
                                 conv_out_w[0].astype(BF16), norm_ffn_g[0:1], wr, br)
    row_t, plan, ys = _moe_experts(h2, logits, exp_gate_w, exp_up_w, exp_down_w, layer=0)

    wr, br = _router_params(router_grp_w[1], router_grp_b[1], router_exp_w[1], router_exp_b[1])
    x2, *qkv = _combine_qkv_proj(row_t, x1, plan, ys, mod[0], mod[1], norm_mix_g[1:2],
                                 attn_in_w[0].astype(BF16))
    outs, lses = zip(*[_dilated_attention(qkv_g) for qkv_g in qkv])
    x3, h2, logits = _attn_out(outs, lses, x2, mod[1], attn_out_w[0].astype(BF16), norm_ffn_g[1:2],
                               wr, br)
    row_t, plan, ys = _moe_experts(h2, logits, exp_gate_w, exp_up_w, exp_down_w, layer=1)
    return _combine_final(row_t, x3, plan, mod[1], g_fin, ys)
```

```python
import functools

import jax
import jax.numpy as jnp
from jax import lax
from jax.experimental import pallas as pl
from jax.experimental.pallas import tpu as pltpu

CONV_WIDTH = 3
DILATED_GROUPS = ((128, 1), (512, 4), (2048, 16))
N_DIL_GROUPS = len(DILATED_GROUPS)
ATTN_HEADS = 8
HEAD_DIM = 64
ATTN_OUT_DIM = ATTN_HEADS * HEAD_DIM
ATTN_IN_DIM = N_DIL_GROUPS * 3 * ATTN_OUT_DIM
ROT_DIM = HEAD_DIM // 4
ROPE_THETA = 500000.0
N_GROUPS = 4
EXPERTS_PER_GROUP = 8
N_EXPERTS = N_GROUPS * EXPERTS_PER_GROUP
NORM_EPS = 1e-6
ATTN_SPAN = 128

V7X_LANES = 128
V7X_VMEM_LIMIT_BYTES = 56 * 1024 * 1024

ROW_TILE = 512
CONV_COL_BLOCK = 256
MIXER_ROW_CHUNKS = 2
ROUTE_TILE = 1024
MOE_BLOCK = 512
EXPERT_ROW_CHUNKS = 2
ATTN_Q_TILE = 1024
ATTN_ITEM_GROUP = 16
ROUTER_LANE_OFFSET = 8
DMA_ROW_TILE = 1024

F32 = jnp.float32
BF16 = jnp.bfloat16


def _params(n_axes):
    return pltpu.CompilerParams(dimension_semantics=("arbitrary",) * n_axes,
                                vmem_limit_bytes=V7X_VMEM_LIMIT_BYTES)


def _norm_mod(x, g, shift, scale):
    ms = jnp.mean(x * x, axis=-1, keepdims=True)
    y = x * lax.rsqrt(ms + NORM_EPS) * g
    return y * (1.0 + scale) + shift


def _pack_bf16_pair(a, b):
    return pltpu.pack_elementwise([a, b], packed_dtype=BF16)


def _unpack_bf16_pair(w):
    a = pltpu.unpack_elementwise(w, index=0, packed_dtype=BF16, unpacked_dtype=F32)
    b = pltpu.unpack_elementwise(w, index=1, packed_dtype=BF16, unpacked_dtype=F32)
    return a, b


def _store_row_groups(ref, x, first=0):
    rows, d = x.shape
    nj = d // (2 * V7X_LANES)
    for j in range(nj):
        lo = x[:, j * V7X_LANES:(j + 1) * V7X_LANES]
        hi = x[:, d // 2 + j * V7X_LANES:d // 2 + (j + 1) * V7X_LANES]
        ref[pl.ds(first * nj + j, rows, stride=nj), :] = _pack_bf16_pair(lo, hi)


def _load_row_groups(ref, rows, nj, first=0):
    los, his = [], []
    for j in range(nj):
        lo, hi = _unpack_bf16_pair(ref[pl.ds(first * nj + j, rows, stride=nj), :])
        los.append(lo)
        his.append(hi)
    return los + his


def _slot_rows_block(ts):
    return (2 * ts // V7X_LANES, V7X_LANES)


def _ffn_input_and_logits(x1, mod, g2_ref, wr_ref, br_ref, h2_ref, lg_ref, first=0):
    rows = x1.shape[0]
    h2 = _norm_mod(x1, g2_ref[...], mod[3:4], mod[4:5])
    _store_row_groups(h2_ref, h2, first)
    lg_ref[first:first + rows, :] = (jnp.dot(h2.astype(BF16), wr_ref[...], preferred_element_type=F32)
                                     + br_ref[...])


def _ada_kernel(c_ref, w_ref, b_ref, o_ref):
    c = c_ref[...]
    ca = c * jax.nn.sigmoid(c)
    o_ref[0] = jnp.dot(ca, w_ref[0], preferred_element_type=F32,
                       precision=lax.Precision.HIGHEST) + b_ref[0]


def _ada_modulation(c, ada_w, ada_b):
    depth, d, n = ada_w.shape
    b = c.shape[0]
    tn = d
    return pl.pallas_call(
        _ada_kernel,
        out_shape=jax.ShapeDtypeStruct((depth, b, n), F32),
        grid=(depth, n // tn),
        in_specs=[pl.BlockSpec((b, d), lambda i, j: (0, 0)),
                  pl.BlockSpec((1, d, tn), lambda i, j: (i, 0, j)),
                  pl.BlockSpec((1, 1, tn), lambda i, j: (i, 0, j))],
        out_specs=pl.BlockSpec((1, b, tn), lambda i, j: (i, 0, j)),
        compiler_params=_params(2),
        name="ada_modulation",
    )(c, ada_w, ada_b.reshape(depth, 1, n))


def _conv_mixer_kernel(x_ref, mod_ref, g_ref, win_ref, cw_ref, wout_ref, g2_ref, wr_ref, br_ref,
                       x1_ref, h2_ref, lg_ref, carry_ref, v_ref, *, cb):
    ts, d = x_ref.shape[1], x_ref.shape[2]

    @pl.when(pl.program_id(1) == 0)
    def _():
        carry_ref[...] = jnp.zeros_like(carry_ref)

    mod = mod_ref[0]
    nr = MIXER_ROW_CHUNKS if ts % (8 * MIXER_ROW_CHUNKS) == 0 else 1
    hr = ts // nr
    xs = [x_ref[0, r * hr:(r + 1) * hr, :] for r in range(nr)]
    hs = [_norm_mod(x, g_ref[...], mod[0:1], mod[1:2]).astype(BF16) for x in xs]
    row = lax.broadcasted_iota(jnp.int32, (hr, cb), 0)
    tails = [carry_ref[:, j * cb:(j + 1) * cb] for j in range(d // cb)]
    for r in range(nr):
        for j in range(d // cb):
            lo = j * cb
            bj = jnp.dot(hs[r], win_ref[:, lo:lo + cb], preferred_element_type=F32)
            cj = jnp.dot(hs[r], win_ref[:, d + lo:d + lo + cb], preferred_element_type=F32)
            uj = jnp.dot(hs[r], win_ref[:, 2 * d + lo:2 * d + lo + cb], preferred_element_type=F32)
            z = cj * uj
            prev = tails[j]
            z1 = jnp.where(row == 0, prev[7:8], pltpu.roll(z, 1, 0))
            z2 = jnp.where(row == 0, prev[6:7], jnp.where(row == 1, prev[7:8], pltpu.roll(z, 2, 0)))
            zc = cw_ref[0:1, lo:lo + cb] * z2 + cw_ref[1:2, lo:lo + cb] * z1 + cw_ref[2:3, lo:lo + cb] * z
            tails[j] = z[hr - 8:hr]
            v_ref[r * hr:(r + 1) * hr, lo:lo + cb] = (bj * zc).astype(BF16)
    for j in range(d // cb):
        carry_ref[:, j * cb:(j + 1) * cb] = tails[j]
    for r in range(nr):
        y = jnp.dot(v_ref[r * hr:(r + 1) * hr, :], wout_ref[...], preferred_element_type=F32)
        x1 = xs[r] + mod[2:3] * y
        x1_ref[0, r * hr:(r + 1) * hr, :] = x1
        _ffn_input_and_logits(x1, mod, g2_ref, wr_ref, br_ref, h2_ref, lg_ref, r * hr)


def _conv_mixer(x, mod, g_mix, w_in, conv_w, w_out, g_ffn, wr, br):
    b, s, d = x.shape
    ts = min(ROW_TILE, s)
    cb = min(CONV_COL_BLOCK, d)
    nst = s // ts
    nj = d // (2 * V7X_LANES)
    const = lambda bi, si: (0, 0)
    return pl.pallas_call(
        functools.partial(_conv_mixer_kernel, cb=cb),
        out_shape=(jax.ShapeDtypeStruct((b, s, d), F32),
                   jax.ShapeDtypeStruct((b * s * nj, V7X_LANES), jnp.uint32),
                   jax.ShapeDtypeStruct((b * s, V7X_LANES), F32)),
        grid=(b, nst),
        in_specs=[pl.BlockSpec((1, ts, d), lambda bi, si: (bi, si, 0)),
                  pl.BlockSpec((1, 6, d), lambda bi, si: (bi, 0, 0)),
                  pl.BlockSpec((1, d), const),
                  pl.BlockSpec((d, 3 * d), const),
                  pl.BlockSpec((CONV_WIDTH, d), const),
                  pl.BlockSpec((d, d), const),
                  pl.BlockSpec((1, d), const),
                  pl.BlockSpec((d, V7X_LANES), const),
                  pl.BlockSpec((1, V7X_LANES), const)],
        out_specs=(pl.BlockSpec((1, ts, d), lambda bi, si: (bi, si, 0)),
                   pl.BlockSpec((ts * nj, V7X_LANES), lambda bi, si: (bi * nst + si, 0)),
                   pl.BlockSpec((ts, V7X_LANES), lambda bi, si: (bi * nst + si, 0))),
        scratch_shapes=[pltpu.VMEM((8, d), F32), pltpu.VMEM((ts, d), BF16)],
        compiler_params=_params(2),
        name="conv_mixer",
    )(x, mod, g_mix, w_in, conv_w, w_out, g_ffn, wr, br)


def _route_kernel(lg_ref, tri_ref, plan_ref, cnt_ref, run_ref, pst_ref, *, blk):
    ph, i = pl.program_id(0), pl.program_id(1)
    tr, lanes = lg_ref.shape
    epg = EXPERTS_PER_GROUP

    @pl.when((ph == 0) & (i == 0))
    def _():
        run_ref[...] = jnp.zeros_like(run_ref)

    lt = lg_ref[...].T
    row8 = lax.broadcasted_iota(jnp.int32, (epg, tr), 0).astype(F32)
    neg = jnp.float32(-jnp.inf)
    big = jnp.float32(epg)
    gmask = row8 < N_GROUPS
    gl = jnp.where(gmask, lt[0:epg], neg)
    mg = jnp.max(gl, axis=0, keepdims=True)
    gidx = jnp.min(jnp.where(gl == mg, row8, big), axis=0, keepdims=True)
    den_g = jnp.sum(jnp.where(gmask, jnp.exp(lt[0:epg] - mg), 0.0), axis=0, keepdims=True)
    grp_p = 1.0 / den_g
    sel = lt[ROUTER_LANE_OFFSET:ROUTER_LANE_OFFSET + epg]
    for g in range(1, N_GROUPS):
        sel = jnp.where(gidx == g, lt[ROUTER_LANE_OFFSET + g * epg:ROUTER_LANE_OFFSET + (g + 1) * epg], sel)
    l1 = jnp.max(sel, axis=0, keepdims=True)
    i1 = jnp.min(jnp.where(sel == l1, row8, big), axis=0, keepdims=True)
    sel2 = jnp.where(row8 == i1, neg, sel)
    l2 = jnp.max(sel2, axis=0, keepdims=True)
    i2 = jnp.min(jnp.where(sel2 == l2, row8, big), axis=0, keepdims=True)
    r = jnp.exp(l2 - l1)
    w1 = grp_p / (1.0 + r)
    w2 = grp_p * r / (1.0 + r)
    e1 = gidx * epg + i1
    e2 = gidx * epg + i2
    row_e = lax.broadcasted_iota(jnp.int32, (N_EXPERTS, tr), 0).astype(F32)
    oh1 = (row_e == e1).astype(F32)
    oh2 = (row_e == e2).astype(F32)
    tot1 = jnp.sum(oh1, axis=1, keepdims=True)
    tot2 = jnp.sum(oh2, axis=1, keepdims=True)
    run = run_ref[:, 0:1]

    @pl.when(ph == 0)
    def _():
        new_run = run + tot1 + tot2
        run_ref[...] = jnp.broadcast_to(new_run, run_ref.shape)

        @pl.when(i == pl.num_programs(1) - 1)
        def _():
            cnt = jnp.broadcast_to(new_run, run_ref.shape)
            padded = jnp.floor((cnt + (blk - 1)) / blk) * blk
            rid = lax.broadcasted_iota(jnp.int32, cnt.shape, 0)
            csum = padded
            k = 1
            while k < N_EXPERTS:
                csum = csum + jnp.where(rid >= k, pltpu.roll(csum, k, 0), 0.0)
                k *= 2
            pst_ref[...] = csum - padded
            run_ref[...] = jnp.zeros_like(run_ref)
            lid = lax.broadcasted_iota(jnp.int32, cnt.shape, 1)
            cnt_ref[...] = jnp.where(lid == 0, cnt, jnp.where(lid == 1, csum, jnp.where(lid == 2, padded, 0.0)))

    @pl.when(ph == 1)
    def _():
        base = pst_ref[:, 0:1] + run
        oh = jnp.concatenate([oh1, oh2], axis=0).astype(BF16)
        off = jnp.concatenate([base, base + tot1], axis=0)
        pos1, pos2 = [], []
        for c in range(tr // lanes):
            ohc = oh[:, c * lanes:(c + 1) * lanes]
            before = jnp.dot(ohc, tri_ref[...], preferred_element_type=F32) + off
            ohf = ohc.astype(F32)
            pos1.append(jnp.sum(ohf[:N_EXPERTS] * before[:N_EXPERTS], axis=0, keepdims=True))
            pos2.append(jnp.sum(ohf[N_EXPERTS:] * before[N_EXPERTS:], axis=0, keepdims=True))
            off = off + jnp.sum(ohf, axis=1, keepdims=True)
        run_ref[...] = jnp.broadcast_to(run + tot1 + tot2, run_ref.shape)
        pos1 = jnp.concatenate(pos1, axis=1)
        pos2 = jnp.concatenate(pos2, axis=1)
        head = jnp.where(row8 == 0, pos1, jnp.where(row8 == 1, pos2,
                         jnp.where(row8 == 2, w1, jnp.where(row8 == 3, w2, 0.0))))
        plan_t = jnp.concatenate([head, jnp.zeros((lanes - epg, tr), F32)], axis=0)
        plan_ref[...] = plan_t.T


def _route(logits):
    t, lanes = logits.shape
    tr = min(ROUTE_TILE, t)
    nt = t // tr
    tri = (jnp.arange(lanes)[:, None] < jnp.arange(lanes)[None, :]).astype(BF16)
    return pl.pallas_call(
        functools.partial(_route_kernel, blk=MOE_BLOCK),
        out_shape=(jax.ShapeDtypeStruct((t, lanes), F32),
                   jax.ShapeDtypeStruct((N_EXPERTS, lanes), F32)),
        grid=(2, nt),
        in_specs=[pl.BlockSpec((tr, lanes), lambda ph, i: (i, 0)),
                  pl.BlockSpec((lanes, lanes), lambda ph, i: (0, 0))],
        out_specs=(pl.BlockSpec((tr, lanes), lambda ph, i: (i * ph, 0)),
                   pl.BlockSpec((N_EXPERTS, lanes), lambda ph, i: (0, 0))),
        scratch_shapes=[pltpu.VMEM((N_EXPERTS, lanes), F32), pltpu.VMEM((N_EXPERTS, lanes), F32)],
        compiler_params=_params(2),
        name="route",
    )(logits, tri)


def _dispatch_kernel(last_ref, row_ref, h2_ref, xs_hbm, zero_buf, sem, zero_sem, *, nj):
    ts = h2_ref.shape[0] // nj
    blk_rows = zero_buf.shape[0]

    @pl.when(pl.program_id(0) == 0)
    def _():
        z = jnp.zeros(zero_buf.shape, F32)
        zero_buf[...] = _pack_bf16_pair(z, z)

        def blank(e):
            dst = xs_hbm.at[pl.ds(pl.multiple_of(last_ref[e], nj), blk_rows)]
            return pltpu.make_async_copy(zero_buf, dst, zero_sem)

        def zstart(e, carry):
            @pl.when(last_ref[e] >= 0)
            def _():
                blank(e).start()
            return carry

        def zwait(e, carry):
            @pl.when(last_ref[e] >= 0)
            def _():
                blank(e).wait()
            return carry

        def tail(bi):
            dst = xs_hbm.at[pl.ds(pl.multiple_of(bi * blk_rows, blk_rows), blk_rows)]
            return pltpu.make_async_copy(zero_buf, dst, zero_sem)

        def tstart(bi, carry):
            tail(bi).start()
            return carry

        def twait(bi, carry):
            tail(bi).wait()
            return carry

        n_blocks = xs_hbm.shape[0] // blk_rows
        lax.fori_loop(0, N_EXPERTS, zstart, 0)
        lax.fori_loop(last_ref[N_EXPERTS], n_blocks, tstart, 0)
        lax.fori_loop(0, N_EXPERTS, zwait, 0)
        lax.fori_loop(last_ref[N_EXPERTS], n_blocks, twait, 0)

    chunk = row_ref.shape[1]

    def start(c, carry):
        t0 = pl.multiple_of(c * chunk, chunk)
        for u in range(chunk):
            src = h2_ref.at[pl.ds(pl.multiple_of((t0 + u) * nj, nj), nj)]
            for k in range(2):
                pltpu.make_async_copy(src, xs_hbm.at[pl.ds(pl.multiple_of(row_ref[2 * c + k, u], nj), nj)],
                                      sem).start(priority=k)
        return carry

    lax.fori_loop(0, ts // chunk, start, 0)
    for _ in range(2):
        pltpu.make_async_copy(h2_ref, xs_hbm.at[pl.ds(0, ts * nj)], sem).wait()


def _dispatch(last_row, row_t, h2, n_rows, nj):
    t = h2.shape[0] // nj
    ts = min(DMA_ROW_TILE, t)
    grid_spec = pltpu.PrefetchScalarGridSpec(
        num_scalar_prefetch=1,
        grid=(t // ts,),
        in_specs=[pl.BlockSpec(_slot_rows_block(ts), lambda i, last: (i, 0), memory_space=pltpu.SMEM),
                  pl.BlockSpec((ts * nj, V7X_LANES), lambda i, last: (i, 0))],
        out_specs=pl.BlockSpec(memory_space=pl.ANY),
        scratch_shapes=[pltpu.VMEM((MOE_BLOCK * nj, V7X_LANES), h2.dtype),
                        pltpu.SemaphoreType.DMA(()), pltpu.SemaphoreType.DMA(())])
    return pl.pallas_call(
        functools.partial(_dispatch_kernel, nj=nj),
        out_shape=jax.ShapeDtypeStruct((n_rows * nj, V7X_LANES), h2.dtype),
        grid_spec=grid_spec,
        compiler_params=_params(1),
        name="dispatch",
    )(last_row, row_t, h2)


def _experts_kernel(be_ref, nact_ref, xs_ref, wg_ref, wu_ref, wd_ref, ys_ref, wgb, wub, wdb, *, nj):
    i = pl.program_id(0)
    blk = xs_ref.shape[0] // nj

    @pl.when(i < nact_ref[0])
    def _():
        prev = be_ref[jnp.maximum(i - 1, 0)]

        @pl.when((i == 0) | (be_ref[i] != prev))
        def _():
            wgb[...] = wg_ref[0, 0].astype(BF16)
            wub[...] = wu_ref[0, 0].astype(BF16)
            wdb[...] = wd_ref[0, 0].astype(BF16)

        nchunk = EXPERT_ROW_CHUNKS if blk % (8 * EXPERT_ROW_CHUNKS) == 0 else 1
        rows = blk // nchunk
        xs = [jnp.concatenate([p.astype(BF16) for p in _load_row_groups(xs_ref, rows, nj, c * rows)], axis=1)
              for c in range(nchunk)]
        gs = [jnp.dot(x, wgb[...], preferred_element_type=F32) for x in xs]
        us = [jnp.dot(x, wub[...], preferred_element_type=F32) for x in xs]
        acts = [(g * jax.nn.sigmoid(g) * u).astype(BF16) for g, u in zip(gs, us)]
        ys = [jnp.dot(a, wdb[...], preferred_element_type=F32) for a in acts]
        for c, y in enumerate(ys):
            _store_row_groups(ys_ref, y, c * rows)

    @pl.when(i >= nact_ref[0])
    def _():
        z = jnp.zeros(ys_ref.shape, F32)
        ys_ref[...] = _pack_bf16_pair(z, z)


def _experts(block_e, n_active, xs, w_gate, w_up, w_down, layer):
    _, _, d, de = w_gate.shape
    nj = d // (2 * V7X_LANES)
    n_blocks = xs.shape[0] // (MOE_BLOCK * nj)
    grid_spec = pltpu.PrefetchScalarGridSpec(
        num_scalar_prefetch=2,
        grid=(n_blocks,),
        in_specs=[pl.BlockSpec((MOE_BLOCK * nj, V7X_LANES), lambda i, be, na: (i, 0)),
                  pl.BlockSpec((1, 1, d, de), lambda i, be, na: (layer, be[i], 0, 0)),
                  pl.BlockSpec((1, 1, d, de), lambda i, be, na: (layer, be[i], 0, 0)),
                  pl.BlockSpec((1, 1, de, d), lambda i, be, na: (layer, be[i], 0, 0))],
        out_specs=pl.BlockSpec((MOE_BLOCK * nj, V7X_LANES), lambda i, be, na: (i, 0)),
        scratch_shapes=[pltpu.VMEM((d, de), BF16), pltpu.VMEM((d, de), BF16),
                        pltpu.VMEM((de, d), BF16)])
    return pl.pallas_call(
        functools.partial(_experts_kernel, nj=nj),
        out_shape=jax.ShapeDtypeStruct(xs.shape, jnp.uint32),
        grid_spec=grid_spec,
        compiler_params=_params(1),
        name="experts",
    )(block_e, n_active, xs, w_gate, w_up, w_down)


def _start_row_gathers(row_ref, ys_hbm, buf, sem, ts, nj):
    chunk = row_ref.shape[1]

    def start(c, carry):
        t0 = pl.multiple_of(c * chunk, chunk)
        for u in range(chunk):
            dst = pl.ds(pl.multiple_of((t0 + u) * nj, nj), nj)
            for k in range(2):
                pltpu.make_async_copy(ys_hbm.at[pl.ds(pl.multiple_of(row_ref[2 * c + k, u], nj), nj)],
                                      buf.at[k, dst], sem).start(priority=k)
        return carry

    lax.fori_loop(0, ts // chunk, start, 0)


def _wait_row_gathers(ys_hbm, buf, sem, ts, nj):
    for k in range(2):
        pltpu.make_async_copy(ys_hbm.at[pl.ds(0, ts * nj)], buf.at[k], sem).wait()


def _weighted_expert_rows(buf, plan_ref, ts, nj):
    w1 = plan_ref[:, 2:3]
    w2 = plan_ref[:, 3:4]
    p1 = _load_row_groups(buf.at[0], ts, nj)
    p2 = _load_row_groups(buf.at[1], ts, nj)
    return jnp.concatenate([w1 * a + w2 * b for a, b in zip(p1, p2)], axis=1)


def _gathered_expert_rows(step, n_steps, rcur_ref, rnext_ref, plan_ref, ys_hbm, buf, sems, ts, nj):
    slot = lax.rem(step, 2)

    @pl.when(step == 0)
    def _():
        _start_row_gathers(rcur_ref, ys_hbm, buf.at[0], sems.at[0], ts, nj)

    @pl.when(step + 1 < n_steps)
    def _():
        _start_row_gathers(rnext_ref, ys_hbm, buf.at[1 - slot], sems.at[1 - slot], ts, nj)

    _wait_row_gathers(ys_hbm, buf.at[slot], sems.at[slot], ts, nj)
    return _weighted_expert_rows(buf.at[slot], plan_ref, ts, nj)


def _combine_kernel(rcur_ref, rnext_ref, x_ref, plan_ref, mod_ref, gfin_ref, ys_hbm, out_ref, buf, sems, *, nj):
    ts = x_ref.shape[1]
    moe = _gathered_expert_rows(pl.program_id(0), pl.num_programs(0), rcur_ref, rnext_ref, plan_ref,
                                ys_hbm, buf, sems, ts, nj)
    x2 = x_ref[0] + mod_ref[0][5:6] * moe
    ms = jnp.mean(x2 * x2, axis=-1, keepdims=True)
    out_ref[0] = x2 * lax.rsqrt(ms + NORM_EPS) * gfin_ref[...]


def _combine_final(row_t, x1, plan, mod, g_final, ys):
    b, s, d = x1.shape
    ts = min(DMA_ROW_TILE, s)
    nst = s // ts
    n_tiles = b * nst
    nj = d // (2 * V7X_LANES)
    return pl.pallas_call(
        functools.partial(_combine_kernel, nj=nj),
        out_shape=jax.ShapeDtypeStruct((b, s, d), F32),
        grid=(n_tiles,),
        in_specs=[pl.BlockSpec(_slot_rows_block(ts), lambda i: (i, 0), memory_space=pltpu.SMEM),
                  pl.BlockSpec(_slot_rows_block(ts), lambda i: (jnp.minimum(i + 1, n_tiles - 1), 0),
                               memory_space=pltpu.SMEM),
                  pl.BlockSpec((1, ts, d), lambda i: (i // nst, i % nst, 0)),
                  pl.BlockSpec((ts, plan.shape[1]), lambda i: (i, 0)),
                  pl.BlockSpec((1, 6, d), lambda i: (i // nst, 0, 0)),
                  pl.BlockSpec((1, d), lambda i: (0, 0)),
                  pl.BlockSpec(memory_space=pl.ANY)],
        out_specs=pl.BlockSpec((1, ts, d), lambda i: (i // nst, i % nst, 0)),
        scratch_shapes=[pltpu.VMEM((2, 2, ts * nj, V7X_LANES), jnp.uint32),
                        pltpu.SemaphoreType.DMA((2,))],
        compiler_params=_params(1),
        name="combine_final",
    )(row_t, row_t, x1, plan, mod, g_final, ys)


def _moe_experts(h2, logits, w_gate, w_up, w_down, *, layer):
    t = logits.shape[0]
    nj = h2.shape[0] // t
    plan, cnt = _route(logits)
    n_blocks = -(-(t * 2) // MOE_BLOCK) + N_EXPERTS
    seg_end, seg_rows = cnt[:, 1], cnt[:, 2]
    pends = (seg_end * (1.0 / MOE_BLOCK)).astype(jnp.int32)
    block_e = jnp.minimum(jnp.sum(pends[None, :] <= jnp.arange(n_blocks)[:, None], axis=1),
                          N_EXPERTS - 1).astype(jnp.int32)
    n_active = pends[-1:]
    last_row = jnp.where(seg_rows > 0, (seg_end - MOE_BLOCK) * nj, -1.0).astype(jnp.int32)
    last_row = jnp.concatenate([last_row, n_active])
    row_t = (plan[:, 0:2] * nj).astype(jnp.int32).reshape(t // V7X_LANES, V7X_LANES, 2)
    row_t = jnp.transpose(row_t, (0, 2, 1)).reshape(2 * t // V7X_LANES, V7X_LANES)
    xs = _dispatch(last_row, row_t, h2, n_blocks * MOE_BLOCK, nj)
    ys = _experts(block_e, n_active, xs, w_gate, w_up, w_down, layer)
    return row_t, plan, ys


def _qkv_kernel(rcur_ref, rnext_ref, x_ref, plan_ref, mod0_ref, mod_ref, g_ref, w_ref, *refs, nj):
    tabs, ys_hbm, x2_ref, (o0_ref, o1_ref, o2_ref) = refs[:9], refs[9], refs[10], refs[11:14]
    scr, buf, sems = refs[14:17]
    ts, d = x_ref.shape[1], x_ref.shape[2]
    step = pl.program_id(0) * pl.num_programs(1) + pl.program_id(1)
    n_steps = pl.num_programs(0) * pl.num_programs(1)
    moe = _gathered_expert_rows(step, n_steps, rcur_ref, rnext_ref, plan_ref, ys_hbm, buf, sems, ts, nj)
    x2 = x_ref[0] + mod0_ref[0][5:6] * moe
    x2_ref[0] = x2
    mod = mod_ref[0]
    hn = _norm_mod(x2, g_ref[...], mod[0:1], mod[1:2])
    wd = ATTN_OUT_DIM
    rep = wd // V7X_LANES
    nc = d // V7X_LANES
    for g, o_ref in enumerate((o0_ref, o1_ref, o2_ref)):
        dil = DILATED_GROUPS[g][1]
        n = ts // dil
        if dil == 1:
            h = hn.astype(BF16)
        else:
            if g == 1:
                for cc in range(nc):
                    scr[cc] = hn[:, cc * V7X_LANES:(cc + 1) * V7X_LANES]
            h = jnp.concatenate(
                [jnp.concatenate([scr[cc, pl.ds(r, n, stride=dil), :] for cc in range(nc)], axis=1)
                 for r in range(dil)], axis=0).astype(BF16)
        cos, sa, sb = (jnp.tile(t[...], (1, rep)) for t in tabs[3 * g:3 * g + 3])
        for j in range(3):
            slab = g * 3 + j
            acc = jnp.dot(h, w_ref[:, slab * wd:(slab + 1) * wd], preferred_element_type=F32)
            if j != 2:
                acc = (acc * cos + pltpu.roll(acc, ROT_DIM // 2, 1) * sa
                       + pltpu.roll(acc, wd - ROT_DIM // 2, 1) * sb)
            acc = acc.astype(BF16)
            for r in range(dil):
                o_ref[0, r, :, j * wd:(j + 1) * wd] = acc[r * n:(r + 1) * n]


def _rope_tables(s, ts):
    half = ROT_DIM // 2
    pos = jnp.arange(s, dtype=F32)
    inv_freq = jnp.power(jnp.float32(ROPE_THETA), -jnp.arange(0, ROT_DIM, 2, dtype=F32) / ROT_DIM)
    ang = pos[:, None] * inv_freq[None, :]
    cos, sin = jnp.cos(ang), jnp.sin(ang)
    m = jnp.arange(V7X_LANES) % HEAD_DIM
    cos_l = jnp.where(m[None, :] < ROT_DIM, cos[:, m % half], 1.0)
    sa_l = jnp.where((m[None, :] >= half) & (m[None, :] < ROT_DIM), sin[:, m % half], 0.0)
    sb_l = jnp.where(m[None, :] < half, -sin[:, m % half], 0.0)
    tabs = []
    for _, dil in DILATED_GROUPS:
        for t in (cos_l, sa_l, sb_l):
            t = t.astype(F32).reshape(s // ts, ts // dil, dil, V7X_LANES)
            tabs.append(jnp.transpose(t, (0, 2, 1, 3)).reshape(s, V7X_LANES))
    return tabs


def _combine_qkv_proj(row_t, x1, plan, ys, mod0, mod, g_mix, w_in):
    b, s, d = x1.shape
    ts = min(ROW_TILE, s)
    nst = s // ts
    nj = d // (2 * V7X_LANES)
    const = lambda si, bi: (0, 0)
    tab = pl.BlockSpec((ts, V7X_LANES), lambda si, bi: (si, 0))
    wd3 = 3 * ATTN_OUT_DIM

    def next_tile(si, bi):
        nb = lax.rem(bi + 1, b)
        ns = jnp.minimum(si + lax.div(bi + 1, b), nst - 1)
        return (nb * nst + ns, 0)

    return pl.pallas_call(
        functools.partial(_qkv_kernel, nj=nj),
        out_shape=(jax.ShapeDtypeStruct((b, s, d), F32),)
                  + tuple(jax.ShapeDtypeStruct((b, dil, s // dil, wd3), BF16) for _, dil in DILATED_GROUPS),
        grid=(nst, b),
        in_specs=[pl.BlockSpec(_slot_rows_block(ts), lambda si, bi: (bi * nst + si, 0),
                               memory_space=pltpu.SMEM),
                  pl.BlockSpec(_slot_rows_block(ts), next_tile, memory_space=pltpu.SMEM),
                  pl.BlockSpec((1, ts, d), lambda si, bi: (bi, si, 0)),
                  pl.BlockSpec((ts, plan.shape[1]), lambda si, bi: (bi * nst + si, 0)),
                  pl.BlockSpec((1, 6, d), lambda si, bi: (bi, 0, 0)),
                  pl.BlockSpec((1, 6, d), lambda si, bi: (bi, 0, 0)),
                  pl.BlockSpec((1, d), const),
                  pl.BlockSpec((d, ATTN_IN_DIM), const)] + [tab] * 9
                 + [pl.BlockSpec(memory_space=pl.ANY)],
        out_specs=(pl.BlockSpec((1, ts, d), lambda si, bi: (bi, si, 0)),)
                  + tuple(pl.BlockSpec((1, dil, ts // dil, wd3), lambda si, bi: (bi, 0, si, 0))
                          for _, dil in DILATED_GROUPS),
        scratch_shapes=[pltpu.VMEM((d // V7X_LANES, ts, V7X_LANES), F32),
                        pltpu.VMEM((2, 2, ts * nj, V7X_LANES), jnp.uint32),
                        pltpu.SemaphoreType.DMA((2,))],
        compiler_params=_params(2),
        name="combine_qkv_proj",
    )(row_t, row_t, x1, plan, mod0, mod, g_mix, w_in, *_rope_tables(s, ts), ys)


def _attn_kernel(q_ref, kc_ref, kp_ref, vc_ref, vp_ref, o_ref, lse_ref):
    first_tile = pl.program_id(2) == 0
    for rr in range(q_ref.shape[1]):
        o, lse = _attend_stream(q_ref[0, rr], kc_ref[0, rr], kp_ref[0, rr], vc_ref[0, rr], vp_ref[0, rr],
                                first_tile, lse_ref.shape[3])
        o_ref[0, rr] = o
        lse_ref[0, rr] = lse


def _attend_stream(q, kc, kp, vc, vp, first_tile, nl):
    tq = q.shape[0]
    sp = ATTN_SPAN
    kext = jnp.concatenate([kp, kc], axis=0)
    vext = jnp.concatenate([vp, vc], axis=0)
    q_t = (q.astype(F32) * (HEAD_DIM ** -0.5)).T.astype(BF16)
    v_t = vext.astype(F32).T.astype(BF16)
    kj = lax.broadcasted_iota(jnp.int32, (2 * sp, sp), 0)
    qi = lax.broadcasted_iota(jnp.int32, (2 * sp, sp), 1)
    neg = jnp.float32(-jnp.inf)
    band = jnp.where(kj >= qi, jnp.where(kj <= qi + sp, 0.0, neg), neg)
    zeros_half = jnp.zeros((HEAD_DIM, sp), BF16)
    head_row = lax.broadcasted_iota(jnp.int32, (ATTN_HEADS, sp), 0)
    nsub = tq // sp
    biases = [jnp.where(first_tile & (kj < sp), neg, band)] + [band] * (nsub - 1)
    o_rows = [[] for _ in range(nsub)]
    lse_ts = [jnp.zeros((ATTN_HEADS, sp), F32) for _ in range(nsub)]
    items = [(i, h) for i in range(nsub) for h in range(ATTN_HEADS)]
    for g0 in range(0, len(items), ATTN_ITEM_GROUP):
        group = items[g0:g0 + ATTN_ITEM_GROUP]
        scores = []
        for i, h in group:
            pair = h // 2
            k_pair = kext[i * sp:(i + 2) * sp, pair * 2 * HEAD_DIM:(pair + 1) * 2 * HEAD_DIM]
            q_h = q_t[h * HEAD_DIM:(h + 1) * HEAD_DIM, i * sp:(i + 1) * sp]
            q_m = jnp.concatenate([q_h, zeros_half] if h % 2 == 0 else [zeros_half, q_h], axis=0)
            scores.append(jnp.dot(k_pair, q_m, preferred_element_type=F32) + biases[i])
        probs, dens = [], []
        for (i, h), sc in zip(group, scores):
            m = jnp.max(sc, axis=0, keepdims=True)
            p = jnp.exp(sc - m)
            den = jnp.sum(p, axis=0, keepdims=True)
            probs.append(p.astype(BF16))
            dens.append(den)
            lse_ts[i] = jnp.where(head_row == h, m + jnp.log(den), lse_ts[i])
        for (i, h), p, den in zip(group, probs, dens):
            v_h = v_t[h * HEAD_DIM:(h + 1) * HEAD_DIM, i * sp:(i + 2) * sp]
            o_rows[i].append(jnp.dot(v_h, p, preferred_element_type=F32) / den)
    out_cols = [jnp.concatenate(rows, axis=0) for rows in o_rows]
    lse_cols = [jnp.concatenate([t, jnp.zeros((nl - ATTN_HEADS, sp), F32)], axis=0) for t in lse_ts]
    return jnp.concatenate(out_cols, axis=1).T.astype(BF16), jnp.concatenate(lse_cols, axis=1).T


def _dilated_attention(qkv_g):
    b, dil, L, _ = qkv_g.shape
    tq = min(ATTN_Q_TILE, L)
    sub = tq // ATTN_SPAN
    wd = ATTN_OUT_DIM
    nres = ATTN_Q_TILE // tq if dil % (ATTN_Q_TILE // tq) == 0 else 1

    def cur(j):
        return pl.BlockSpec((1, nres, tq, wd), lambda bi, r, lb: (bi, r, lb, j))

    def prev(j):
        return pl.BlockSpec((1, nres, ATTN_SPAN, wd),
                            lambda bi, r, lb: (bi, r, jnp.maximum(lb * sub - 1, 0), j))

    return pl.pallas_call(
        _attn_kernel,
        out_shape=(jax.ShapeDtypeStruct((b, dil, L, wd), BF16),
                   jax.ShapeDtypeStruct((b, dil, L, V7X_LANES), F32)),
        grid=(b, dil // nres, L // tq),
        in_specs=[cur(0), cur(1), prev(1), cur(2), prev(2)],
        out_specs=(pl.BlockSpec((1, nres, tq, wd), lambda bi, r, lb: (bi, r, lb, 0)),
                   pl.BlockSpec((1, nres, tq, V7X_LANES), lambda bi, r, lb: (bi, r, lb, 0))),
        compiler_params=_params(3),
        name=f"dilated_attention_{dil}",
    )(qkv_g, qkv_g, qkv_g, qkv_g, qkv_g)


def _attn_out_kernel(o0_ref, o1_ref, o2_ref, l0_ref, l1_ref, l2_ref, x_ref, mod_ref, wout_ref, e_ref,
                     g2_ref, wr_ref, br_ref, x1_ref, h2_ref, lg_ref, o_scr, l_scr):
    ts = x_ref.shape[1]
    nc = ATTN_OUT_DIM // V7X_LANES
    for g, (o_ref, l_ref) in enumerate(((o0_ref, l0_ref), (o1_ref, l1_ref), (o2_ref, l2_ref))):
        dil = DILATED_GROUPS[g][1]
        for r in range(dil):
            rows = pl.ds(r, ts // dil, stride=dil) if dil > 1 else slice(None)
            o_r = o_ref[0, r].astype(F32)
            for cc in range(nc):
                o_scr[g, cc, rows, :] = o_r[:, cc * V7X_LANES:(cc + 1) * V7X_LANES]
            l_scr[g, rows, :] = l_ref[0, r]
    l0, l1, l2 = l_scr[0], l_scr[1], l_scr[2]
    m = jnp.maximum(jnp.maximum(l0, l1), l2)
    es = (jnp.exp(l0 - m), jnp.exp(l1 - m), jnp.exp(l2 - m))
    den = es[0] + es[1] + es[2]
    merged = jnp.zeros((ts, ATTN_OUT_DIM), F32)
    for g in range(N_DIL_GROUPS):
        w = es[g] / den
        w_hi = w.astype(BF16)
        w_lo = (w - w_hi.astype(F32)).astype(BF16)
        wfull = (jnp.dot(w_hi, e_ref[...], preferred_element_type=F32)
                 + jnp.dot(w_lo, e_ref[...], preferred_element_type=F32))
        merged = merged + wfull * jnp.concatenate([o_scr[g, cc] for cc in range(nc)], axis=1)
    mod = mod_ref[0]
    merged = merged.astype(BF16)
    nr = MIXER_ROW_CHUNKS if ts % (8 * MIXER_ROW_CHUNKS) == 0 else 1
    hr = ts // nr
    for r in range(nr):
        y = jnp.dot(merged[r * hr:(r + 1) * hr], wout_ref[...], preferred_element_type=F32)
        x1 = x_ref[0, r * hr:(r + 1) * hr, :] + mod[2:3] * y
        x1_ref[0, r * hr:(r + 1) * hr, :] = x1
        _ffn_input_and_logits(x1, mod, g2_ref, wr_ref, br_ref, h2_ref, lg_ref, r * hr)


def _attn_out(outs, lses, x, mod, w_out, g_ffn, wr, br):
    b, s, d = x.shape
    ts = min(ROW_TILE, s)
    nst = s // ts
    nj = d // (2 * V7X_LANES)
    const = lambda bi, si: (0, 0)
    tok = lambda bi, si: (bi * nst + si, 0)
    stream = lambda bi, si: (bi, 0, si, 0)
    expand = (jnp.arange(V7X_LANES)[:, None] == jnp.arange(ATTN_OUT_DIM)[None, :] // HEAD_DIM).astype(BF16)
    return pl.pallas_call(
        _attn_out_kernel,
        out_shape=(jax.ShapeDtypeStruct((b, s, d), F32),
                   jax.ShapeDtypeStruct((b * s * nj, V7X_LANES), jnp.uint32),
                   jax.ShapeDtypeStruct((b * s, V7X_LANES), F32)),
        grid=(b, nst),
        in_specs=[pl.BlockSpec((1, dil, ts // dil, ATTN_OUT_DIM), stream) for _, dil in DILATED_GROUPS]
                 + [pl.BlockSpec((1, dil, ts // dil, V7X_LANES), stream) for _, dil in DILATED_GROUPS]
                 + [pl.BlockSpec((1, ts, d), lambda bi, si: (bi, si, 0)),
                    pl.BlockSpec((1, 6, d), lambda bi, si: (bi, 0, 0)),
                    pl.BlockSpec((ATTN_OUT_DIM, d), const),
                    pl.BlockSpec((V7X_LANES, ATTN_OUT_DIM), const),
                    pl.BlockSpec((1, d), const),
                    pl.BlockSpec((d, V7X_LANES), const),
                    pl.BlockSpec((1, V7X_LANES), const)],
        out_specs=(pl.BlockSpec((1, ts, d), lambda bi, si: (bi, si, 0)),
                   pl.BlockSpec((ts * nj, V7X_LANES), tok),
                   pl.BlockSpec((ts, V7X_LANES), tok)),
        scratch_shapes=[pltpu.VMEM((N_DIL_GROUPS, ATTN_OUT_DIM // V7X_LANES, ts, V7X_LANES), F32),
                        pltpu.VMEM((N_DIL_GROUPS, ts, V7X_LANES), F32)],
        compiler_params=_params(2),
        name="attn_out",
    )(*outs, *lses, x, mod, w_out, expand, g_ffn, wr, br)


def _router_params(w_grp, b_grp, w_exp, b_exp):
    d = w_grp.shape[0]
    w_e = jnp.transpose(w_exp, (1, 0, 2)).reshape(d, N_EXPERTS)
    pad1 = ROUTER_LANE_OFFSET - N_GROUPS
    pad2 = V7X_LANES - ROUTER_LANE_OFFSET - N_EXPERTS
    wr = jnp.concatenate([w_grp, jnp.zeros((d, pad1), F32), w_e, jnp.zeros((d, pad2), F32)], axis=1)
    br = jnp.concatenate([b_grp, jnp.zeros((pad1,), F32), b_exp.reshape(-1), jnp.zeros((pad2,), F32)])
    return wr.astype(BF16), br.reshape(1, V7X_LANES).astype(F32)


def kernel(x, c, norm_mix_g, norm_ffn_g, ada_w, ada_b, conv_in_w, conv_w, conv_out_w, attn_in_w,
           attn_out_w, router_grp_w, router_grp_b, router_exp_w, router_exp_b, exp_gate_w, exp_up_w,
           exp_down_w, final_norm_g):
    b, s, d = x.shape
    depth = ada_w.shape[0]
    assert depth == 2 and s % (DILATED_GROUPS[-1][1] * ATTN_SPAN) == 0 and d % (2 * V7X_LANES) == 0
    mod = _ada_modulation(c, ada_w, ada_b).reshape(depth, b, 6, d)
    g_fin = final_norm_g.reshape(1, d)

    wr, br = _router_params(router_grp_w[0], router_grp_b[0], router_exp_w[0], router_exp_b[0])
    x1, h2, logits = _conv_mixer(x, mod[0], norm_mix_g[0:1], conv_in_w[0].astype(BF16), conv_w[0],
                                 conv_out_w[0].astype(BF16), norm_ffn_g[0:1], wr, br)
    row_t, plan, ys = _moe_experts(h2, logits, exp_gate_w, exp_up_w, exp_down_w, layer=0)

    wr, br = _router_params(router_grp_w[1], router_grp_b[1], router_exp_w[1], router_exp_b[1])
    x2, *qkv = _combine_qkv_proj(row_t, x1, plan, ys, mod[0], mod[1], norm_mix_g[1:2],
                                 attn_in_w[0].astype(BF16))
    outs, lses = zip(*[_dilated_attention(qkv_g) for qkv_g in qkv])
    x3, h2, logits = _attn_out(outs, lses, x2, mod[1], attn_out_w[0].astype(BF16), norm_ffn_g[1:2],
                               wr, br)
    row_t, plan, ys = _moe_experts(h2, logits, exp_gate_w, exp_up_w, exp_down_w, layer=1)
    return _combine_final(row_t, x3, plan, mod[1], g_fin, ys)
```
